```python
import jax
import jax.numpy as jnp
from jax import lax
import numpy as np

D_MODEL = 4096
BATCH = 4
SEQ = 2048
DEPTH = 4
DEC_BATCH = 8
DEC_SEQ = 1
PAST_LEN = 8192
PAGE_SIZE = 128

N_EVEN = (DEPTH + 1) // 2
N_ODD = DEPTH // 2
W_A = D_MODEL // 2
W_B = D_MODEL // 2
CHUNK = 128
A_GROUP_CH = 128
A_GROUPS = W_A // A_GROUP_CH
CONV_W = 31
W_C = D_MODEL
HEAD_DIM = 128
N_HEADS_C = W_C // HEAD_DIM
Q_BLOCK = 128
SB_BIAS_HI = 4.0
SB_BIAS_LO = 9.0
EPS = 1e-6
EVEN_IN = 3 * W_A + 3 * W_B
ODD_IN = 4 * W_C

kernel_name = "hybrid_gmlp_conformer_stickbreak_step"


def _rmsnorm(x, g):
    xf = x.astype(jnp.float32)
    y = xf * lax.rsqrt(jnp.mean(xf * xf, axis=-1, keepdims=True) + EPS)
    return (y * g.astype(jnp.float32)).astype(x.dtype)


def _layernorm(x, g, b):
    xf = x.astype(jnp.float32)
    xc = xf - jnp.mean(xf, axis=-1, keepdims=True)
    y = xc * lax.rsqrt(jnp.mean(xc * xc, axis=-1, keepdims=True) + EPS)
    return (y * g.astype(jnp.float32) + b.astype(jnp.float32)).astype(x.dtype)


def _spatial_gate(v, ws, wb):
    bsz, t = v.shape[0], v.shape[1]
    n = min(t, CHUNK)
    nc = t // n
    w = jnp.tril(ws[:, :n, :n])
    vc = v.reshape(bsz, nc, n, A_GROUPS, A_GROUP_CH)
    out = jnp.einsum('gts,bnsgc->bntgc', w, vc) + wb[:, :n].T[None, None, :, :, None]
    return out.reshape(bsz, t, W_A)


def _causal_dwconv(x, prefix, w, bias):
    xp = jnp.concatenate([prefix, x], axis=1)
    y = lax.conv_general_dilated(xp, w[:, None, :], window_strides=(1,), padding='VALID',
                                 dimension_numbers=('NWC', 'WIO', 'NWC'), feature_group_count=W_B)
    return y + bias, xp[:, xp.shape[1] - (CONV_W - 1):]


def _even_mixer(h, conv_prefix, w_in, a_ln_g, a_ln_b, a_ws, a_wb, b_dw, b_dw_bias, b_ln_g, b_ln_b, b_pw, w_out):
    bsz, t, _ = h.shape
    z = h @ w_in
    a_u, a_v, a_g, b_a, b_b, b_g = jnp.split(
        z, [W_A, 2 * W_A, 3 * W_A, 3 * W_A + W_B, 3 * W_A + 2 * W_B], axis=-1)
    v_n = _layernorm(jax.nn.gelu(a_v), a_ln_g, a_ln_b)
    a_sp = _spatial_gate(v_n.reshape(bsz, t, A_GROUPS, A_GROUP_CH), a_ws, a_wb)
    a_out = jax.nn.gelu(a_u) * a_sp * jax.nn.silu(a_g)
    glu = b_a * jax.nn.sigmoid(b_b)
    conv, new_buf = _causal_dwconv(glu, conv_prefix, b_dw, b_dw_bias)
    c = jax.nn.silu(_layernorm(conv, b_ln_g, b_ln_b))
    b_out = (c @ b_pw) * jax.nn.silu(b_g)
    y = jnp.concatenate([a_out, b_out], axis=-1) @ w_out
    return y, v_n, new_buf


def _qkvg(h, w_in, q_g, k_g):
    bsz, t, _ = h.shape
    q, k, v, g = jnp.split(h @ w_in, 4, axis=-1)
    shp = (bsz, t, N_HEADS_C, HEAD_DIM)
    q = _rmsnorm(q.reshape(shp), q_g)
    k = _rmsnorm(k.reshape(shp), k_g)
    return q, k, v.reshape(shp), g


def _stick_break(q, k, v, q_pos, bias):
    z = (jnp.einsum('bqhd,bkhd->bhqk', q, k, preferred_element_type=jnp.float32) * (HEAD_DIM ** -0.5)
         + bias.astype(jnp.float32)[None, :, None, None])
    mask = jnp.arange(k.shape[1], dtype=jnp.int32)[None, :] < q_pos[:, None]
    log_keep = jnp.where(mask, jax.nn.log_sigmoid(-z), 0.0)
    after = lax.cumsum(log_keep, axis=3, reverse=True) - log_keep
    w = jnp.where(mask, jnp.exp(jax.nn.log_sigmoid(z) + after), 0.0)
    o = jnp.einsum('bhqk,bkhd->bqhd', w, v.astype(jnp.float32))
    return o.astype(q.dtype)


def _odd_prompt(h, w_in, q_g, k_g, bias, w_out):
    bsz, t, _ = h.shape
    q, k, v, g = _qkvg(h, w_in, q_g, k_g)
    nb = t // Q_BLOCK
    qb = jnp.swapaxes(q.reshape(bsz, nb, Q_BLOCK, N_HEADS_C, HEAD_DIM), 0, 1)
    pos = jnp.arange(t, dtype=jnp.int32).reshape(nb, Q_BLOCK)
    o = lax.map(lambda a: _stick_break(a[0], k, v, a[1], bias), (qb, pos))
    o = jnp.swapaxes(o, 0, 1).reshape(bsz, t, W_C)
    return (o * jax.nn.silu(g)) @ w_out, k, v


def _odd_sample(h, k_pages, v_pages, w_in, q_g, k_g, bias, w_out):
    bsz, t, _ = h.shape
    past = k_pages.shape[1] * PAGE_SIZE
    q, k, v, g = _qkvg(h, w_in, q_g, k_g)
    k_all = jnp.concatenate([k_pages.reshape(bsz, past, N_HEADS_C, HEAD_DIM), k], axis=1)
    v_all = jnp.concatenate([v_pages.reshape(bsz, past, N_HEADS_C, HEAD_DIM), v], axis=1)
    q_pos = past + jnp.arange(t, dtype=jnp.int32)
    o = _stick_break(q, k_all, v_all, q_pos, bias).reshape(bsz, t, W_C)
    return (o * jax.nn.silu(g)) @ w_out, k, v


def setup_inputs(seed: int = 0) -> dict:
    key = jax.random.key(seed)
    ks = jax.random.split(key, 24)
    nrm = lambda k, shape, s: jax.random.normal(k, shape, jnp.float32) * s
    n_pages = PAST_LEN // PAGE_SIZE
    n_used = DEC_BATCH * n_pages
    n_pool = n_used + (n_used + 3) // 4
    page_table = jax.random.permutation(ks[5], n_pool)[:n_used].reshape(DEC_BATCH, n_pages).astype(jnp.int32)
    c_bias = (-jnp.linspace(SB_BIAS_HI, SB_BIAS_LO, N_HEADS_C, dtype=jnp.float32)[None, :]
              + nrm(ks[23], (N_ODD, N_HEADS_C), 0.1))
    return {
        "x_prompt": nrm(ks[0], (BATCH, SEQ, D_MODEL), 1.0),
        "x_sample": nrm(ks[1], (DEC_BATCH, DEC_SEQ, D_MODEL), 1.0),
        "state_conv": nrm(ks[2], (N_EVEN, DEC_BATCH, CONV_W - 1, W_B), 0.5),
        "cache_k": nrm(ks[3], (N_ODD, n_pool, PAGE_SIZE, N_HEADS_C, HEAD_DIM), 1.0),
        "cache_v": nrm(ks[4], (N_ODD, n_pool, PAGE_SIZE, N_HEADS_C, HEAD_DIM), 1.0),
        "page_table": page_table,
        "norm_even": 1.0 + nrm(ks[6], (N_EVEN, D_MODEL), 0.02),
        "w_in_even": nrm(ks[7], (N_EVEN, D_MODEL, EVEN_IN), D_MODEL ** -0.5),
        "a_ln_g": 1.0 + nrm(ks[8], (N_EVEN, W_A), 0.02),
        "a_ln_b": nrm(ks[9], (N_EVEN, W_A), 0.02),
        "a_ws": nrm(ks[10], (N_EVEN, A_GROUPS, CHUNK, CHUNK), CHUNK ** -0.5),
        "a_wb": nrm(ks[11], (N_EVEN, A_GROUPS, CHUNK), 0.02),
        "b_dw": nrm(ks[12], (N_EVEN, CONV_W, W_B), CONV_W ** -0.5),
        "b_dw_bias": nrm(ks[13], (N_EVEN, W_B), 0.02),
        "b_ln_g": 1.0 + nrm(ks[14], (N_EVEN, W_B), 0.02),
        "b_ln_b": nrm(ks[15], (N_EVEN, W_B), 0.02),
        "b_pw": nrm(ks[16], (N_EVEN, W_B, W_B), W_B ** -0.5),
        "w_out_even": nrm(ks[17], (N_EVEN, W_A + W_B, D_MODEL), (W_A + W_B) ** -0.5),
        "norm_odd": 1.0 + nrm(ks[18], (N_ODD, D_MODEL), 0.02),
        "w_in_odd": nrm(ks[19], (N_ODD, D_MODEL, ODD_IN), D_MODEL ** -0.5),
        "q_norm": 1.0 + nrm(ks[20], (N_ODD, HEAD_DIM), 0.02),
        "k_norm": 1.0 + nrm(ks[21], (N_ODD, HEAD_DIM), 0.02),
        "c_bias": c_bias,
        "w_out_odd": nrm(ks[22], (N_ODD, W_C, D_MODEL), W_C ** -0.5),
    }


def reference(x_prompt, x_sample, state_conv, cache_k, cache_v, page_table,
              norm_even, w_in_even, a_ln_g, a_ln_b, a_ws, a_wb, b_dw, b_dw_bias, b_ln_g, b_ln_b, b_pw, w_out_even,
              norm_odd, w_in_odd, q_norm, k_norm, c_bias, w_out_odd):
    xp, xs = x_prompt, x_sample
    a_v_s, conv_p, conv_s, k_p, v_p, k_s, v_s = [], [], [], [], [], [], []
    for layer in range(DEPTH):
        i = layer // 2
        if layer % 2 == 0:
            prm = (a_ln_g[i], a_ln_b[i], a_ws[i], a_wb[i], b_dw[i], b_dw_bias[i],
                   b_ln_g[i], b_ln_b[i], b_pw[i], w_out_even[i])
            zero_prefix = jnp.zeros((xp.shape[0], CONV_W - 1, W_B), xp.dtype)
            yp, _, bp = _even_mixer(_rmsnorm(xp, norm_even[i]), zero_prefix, w_in_even[i], *prm)
            ys, vs, bs = _even_mixer(_rmsnorm(xs, norm_even[i]), state_conv[i], w_in_even[i], *prm)
            a_v_s.append(vs)
            conv_p.append(bp)
            conv_s.append(bs)
        else:
            yp, kp, vp = _odd_prompt(_rmsnorm(xp, norm_odd[i]), w_in_odd[i], q_norm[i], k_norm[i],
                                     c_bias[i], w_out_odd[i])
            ys, kn, vn = _odd_sample(_rmsnorm(xs, norm_odd[i]), cache_k[i][page_table], cache_v[i][page_table],
                                     w_in_odd[i], q_norm[i], k_norm[i], c_bias[i], w_out_odd[i])
            k_p.append(kp)
            v_p.append(vp)
            k_s.append(kn)
            v_s.append(vn)
        xp = xp + yp
        xs = xs + ys
    return (xp, xs, jnp.stack(a_v_s), jnp.stack(conv_p), jnp.stack(conv_s),
            jnp.stack(k_p), jnp.stack(v_p), jnp.stack(k_s), jnp.stack(v_s))
```

```python
import functools
import math

import jax
import jax.numpy as jnp
from jax import lax
from jax.experimental import pallas as pl
from jax.experimental.pallas import tpu as pltpu

D_MODEL = 4096
BATCH = 4
SEQ = 2048
DEPTH = 4
DEC_BATCH = 8
DEC_SEQ = 1
PAST_LEN = 8192
PAGE_SIZE = 128
N_EVEN = (DEPTH + 1) // 2
N_ODD = DEPTH // 2
W_A = D_MODEL // 2
W_B = D_MODEL // 2
CHUNK = 128
A_GROUP_CH = 128
A_GROUPS = W_A // A_GROUP_CH
CONV_W = 31
W_C = D_MODEL
HEAD_DIM = 128
N_HEADS_C = W_C // HEAD_DIM
EPS = 1e-6
EVEN_IN = 3 * W_A + 3 * W_B
ODD_IN = 4 * W_C
N_PAGES = PAST_LEN // PAGE_SIZE

LANES = 128
VMEM_LIMIT = 56 * 1024 * 1024

F32 = jnp.float32
BF16 = jnp.bfloat16

M_PROMPT = BATCH * SEQ
CONV_PAD = 32

assert DEC_SEQ == 1 and CHUNK == LANES and HEAD_DIM == LANES and A_GROUP_CH == LANES


def _gelu(x):
    return 0.5 * x * (1.0 + jnp.tanh(math.sqrt(2.0 / math.pi) * (x + 0.044715 * (x * x * x))))


def _sigmoid(x):
    return 1.0 / (1.0 + jnp.exp(-x))


def _silu(x):
    return x * _sigmoid(x)


def _layernorm(x, g, b):
    xc = x - jnp.mean(x, axis=-1, keepdims=True)
    y = xc * lax.rsqrt(jnp.mean(xc * xc, axis=-1, keepdims=True) + EPS)
    return y * g + b


def _dot(a, b):
    return jnp.dot(a, b, preferred_element_type=F32)


def _dot_nt(a, b):
    return lax.dot_general(a, b, (((1,), (1,)), ((), ())), preferred_element_type=F32)


def _rmsnorm_kernel(x_ref, g_ref, o_ref):
    x = x_ref[...]
    y = x * lax.rsqrt(jnp.mean(x * x, axis=-1, keepdims=True) + EPS)
    o_ref[...] = (y * g_ref[...]).astype(o_ref.dtype)


def _rmsnorm(x, g, tm):
    m, d = x.shape
    return pl.pallas_call(
        _rmsnorm_kernel,
        grid=(m // tm,),
        in_specs=[pl.BlockSpec((tm, d), lambda i: (i, 0)), pl.BlockSpec((1, d), lambda i: (0, 0))],
        out_specs=pl.BlockSpec((tm, d), lambda i: (i, 0)),
        out_shape=jax.ShapeDtypeStruct((m, d), BF16),
        compiler_params=pltpu.CompilerParams(dimension_semantics=("parallel",), vmem_limit_bytes=VMEM_LIMIT),
        name="rmsnorm",
    )(x, g.reshape(1, d))


def _mm_kernel(*refs, n_x, n_extra, epilogue):
    x_refs = refs[:n_x]
    w_refs = refs[n_x:2 * n_x]
    extra_refs = refs[2 * n_x:2 * n_x + n_extra]
    o_ref = refs[2 * n_x + n_extra]
    wbf_refs = refs[2 * n_x + n_extra + 1:]

    @pl.when(pl.program_id(1) == 0)
    def _():
        for w_ref, wbf_ref in zip(w_refs, wbf_refs):
            wbf_ref[...] = w_ref[...].astype(BF16)

    acc = _dot(x_refs[0][...], wbf_refs[0][...])
    for x_ref, wbf_ref in zip(x_refs[1:], wbf_refs[1:]):
        acc = acc + _dot(x_ref[...], wbf_ref[...])
    epilogue(acc, o_ref, *extra_refs)


def _matmul(xs, w, layer, row_blocks, col_off, n, *, tm, tn, epilogue, extras=(), extra_specs=(), out_dtype=F32,
            name="matmul"):
    m = xs[0].shape[0]
    cb = col_off // tn
    in_specs = [pl.BlockSpec((tm, x.shape[1]), lambda j, i: (i, 0)) for x in xs]
    for x, rb in zip(xs, row_blocks):
        in_specs.append(pl.BlockSpec((None, x.shape[1], tn), functools.partial(lambda j, i, rb: (layer, rb, j + cb), rb=rb)))
    in_specs += list(extra_specs)
    kern = functools.partial(_mm_kernel, n_x=len(xs), n_extra=len(extras), epilogue=epilogue)
    return pl.pallas_call(
        kern,
        grid=(n // tn, m // tm),
        in_specs=in_specs,
        out_specs=pl.BlockSpec((tm, tn), lambda j, i: (i, j)),
        out_shape=jax.ShapeDtypeStruct((m, n), out_dtype),
        scratch_shapes=[pltpu.VMEM((x.shape[1], tn), BF16) for x in xs],
        compiler_params=pltpu.CompilerParams(dimension_semantics=("parallel", "arbitrary"),
                                             vmem_limit_bytes=VMEM_LIMIT),
        name=name,
    )(*xs, *([w] * len(xs)), *extras)


def _ep_plain(acc, o_ref):
    o_ref[...] = acc.astype(o_ref.dtype)


def _ep_headnorm(acc, o_ref, g_ref):
    g = g_ref[...]
    for h in range(acc.shape[1] // HEAD_DIM):
        a = acc[:, h * HEAD_DIM:(h + 1) * HEAD_DIM]
        y = a * lax.rsqrt(jnp.mean(a * a, axis=-1, keepdims=True) + EPS)
        o_ref[:, h * HEAD_DIM:(h + 1) * HEAD_DIM] = (y * g).astype(o_ref.dtype)


def _ep_gate(acc, o_ref, gate_ref):
    o_ref[...] = (acc * _silu(gate_ref[...])).astype(o_ref.dtype)


def _ep_resid(acc, o_ref, r_ref):
    o_ref[...] = (r_ref[...] + acc).astype(o_ref.dtype)


def _even_mid_kernel(au_ref, av_ref, ag_ref, ba_ref, bb_ref, alg_ref, alb_ref, ws_ref, wbt_ref, dw_ref, dwb_ref,
                     blg_ref, blb_ref, aout_ref, c_ref, tail_ref, wsb_ref, xp_ref, conv_ref):
    b = pl.program_id(0)
    c = pl.program_id(1)
    n_chunks = pl.num_programs(1)

    @pl.when(jnp.logical_and(b == 0, c == 0))
    def _():
        t_idx = lax.broadcasted_iota(jnp.int32, (CHUNK, CHUNK), 0)
        s_idx = lax.broadcasted_iota(jnp.int32, (CHUNK, CHUNK), 1)
        for g in range(A_GROUPS):
            wsb_ref[g] = jnp.where(s_idx <= t_idx, ws_ref[g], 0.0).astype(BF16)

    v_n = _layernorm(_gelu(av_ref[...]), alg_ref[...], alb_ref[...])
    v_b = v_n.astype(BF16)
    for g in range(A_GROUPS):
        sl = slice(g * A_GROUP_CH, (g + 1) * A_GROUP_CH)
        sp = _dot(wsb_ref[g], v_b[:, sl]) + wbt_ref[:, g:g + 1]
        aout_ref[:, sl] = (_gelu(au_ref[:, sl]) * sp * _silu(ag_ref[:, sl])).astype(aout_ref.dtype)

    @pl.when(c == 0)
    def _():
        xp_ref[0:CONV_PAD, :] = jnp.zeros((CONV_PAD, W_B), F32)

    xp_ref[CONV_PAD:CONV_PAD + CHUNK, :] = ba_ref[...] * _sigmoid(bb_ref[...])
    first = CONV_PAD - (CONV_W - 1)
    for cb in range(W_B // LANES):
        sl = slice(cb * LANES, (cb + 1) * LANES)
        conv = jnp.broadcast_to(dwb_ref[:, sl], (CHUNK, LANES))
        for j in range(CONV_W):
            conv = conv + xp_ref[first + j:first + j + CHUNK, sl] * dw_ref[j:j + 1, sl]
        conv_ref[:, sl] = conv
    c_ref[...] = _silu(_layernorm(conv_ref[...], blg_ref[...], blb_ref[...])).astype(c_ref.dtype)

    @pl.when(c == n_chunks - 1)
    def _():
        tail_ref[...] = xp_ref[CONV_PAD + CHUNK - (CONV_W - 1):CONV_PAD + CHUNK, :]

    xp_ref[0:CONV_PAD, :] = xp_ref[CHUNK:CHUNK + CONV_PAD, :]


def _even_mid_prompt(z, a_ln_g, a_ln_b, a_ws, a_wbt, b_dw, b_dw_bias, b_ln_g, b_ln_b):
    n_chunks = SEQ // CHUNK
    zspec = lambda k: pl.BlockSpec((CHUNK, W_A), functools.partial(lambda b, c, k: (b * n_chunks + c, k), k=k))
    row = lambda w: pl.BlockSpec((1, w), lambda b, c: (0, 0))
    return pl.pallas_call(
        _even_mid_kernel,
        grid=(BATCH, n_chunks),
        in_specs=[zspec(0), zspec(1), zspec(2), zspec(3), zspec(4), row(W_A), row(W_A),
                  pl.BlockSpec((A_GROUPS, CHUNK, CHUNK), lambda b, c: (0, 0, 0)),
                  pl.BlockSpec((CHUNK, A_GROUPS), lambda b, c: (0, 0)),
                  pl.BlockSpec((CONV_W, W_B), lambda b, c: (0, 0)), row(W_B), row(W_B), row(W_B)],
        out_specs=[pl.BlockSpec((CHUNK, W_A), lambda b, c: (b * n_chunks + c, 0)),
                   pl.BlockSpec((CHUNK, W_B), lambda b, c: (b * n_chunks + c, 0)),
                   pl.BlockSpec((None, CONV_W - 1, W_B), lambda b, c: (b, 0, 0))],
        out_shape=[jax.ShapeDtypeStruct((M_PROMPT, W_A), BF16), jax.ShapeDtypeStruct((M_PROMPT, W_B), BF16),
                   jax.ShapeDtypeStruct((BATCH, CONV_W - 1, W_B), F32)],
        scratch_shapes=[pltpu.VMEM((A_GROUPS, CHUNK, CHUNK), BF16), pltpu.VMEM((CONV_PAD + CHUNK, W_B), F32),
                        pltpu.VMEM((CHUNK, W_B), F32)],
        compiler_params=pltpu.CompilerParams(dimension_semantics=("arbitrary", "arbitrary"),
                                             vmem_limit_bytes=VMEM_LIMIT),
        name="even_mid",
    )(z, z, z, z, z, a_ln_g.reshape(1, W_A), a_ln_b.reshape(1, W_A), a_ws, a_wbt, b_dw, b_dw_bias.reshape(1, W_B),
      b_ln_g.reshape(1, W_B), b_ln_b.reshape(1, W_B))


def _even_mid_sample_kernel(z_ref, st_ref, alg_ref, alb_ref, ws0_ref, wb0_ref, dw_ref, dwb_ref, blg_ref, blb_ref,
                            aout_ref, c_ref, vn_ref, nst_ref, conv_ref):
    a_u = z_ref[:, 0:W_A]
    a_v = z_ref[:, W_A:2 * W_A]
    a_g = z_ref[:, 2 * W_A:3 * W_A]
    b_a = z_ref[:, 3 * W_A:3 * W_A + W_B]
    b_b = z_ref[:, 3 * W_A + W_B:3 * W_A + 2 * W_B]
    v_n = _layernorm(_gelu(a_v), alg_ref[...], alb_ref[...])
    vn_ref[...] = v_n
    sp = ws0_ref[...] * v_n + wb0_ref[...]
    aout_ref[...] = (_gelu(a_u) * sp * _silu(a_g)).astype(aout_ref.dtype)
    glu = b_a * _sigmoid(b_b)
    hist = CONV_W - 1
    for b in range(DEC_BATCH):
        st = st_ref[b]
        row = glu[b:b + 1, :]
        conv_ref[b:b + 1, :] = (jnp.sum(st * dw_ref[0:hist, :], axis=0, keepdims=True) + row * dw_ref[hist:CONV_W, :]
                                + dwb_ref[...])
        nst_ref[b, 0:hist - 1, :] = st[1:hist, :]
        nst_ref[b, hist - 1:hist, :] = row
    c_ref[...] = _silu(_layernorm(conv_ref[...], blg_ref[...], blb_ref[...])).astype(c_ref.dtype)


def _even_mid_sample(z, state, a_ln_g, a_ln_b, ws0, wb0, b_dw, b_dw_bias, b_ln_g, b_ln_b):
    return pl.pallas_call(
        _even_mid_sample_kernel,
        out_shape=[jax.ShapeDtypeStruct((DEC_BATCH, W_A), BF16), jax.ShapeDtypeStruct((DEC_BATCH, W_B), BF16),
                   jax.ShapeDtypeStruct((DEC_BATCH, W_A), F32),
                   jax.ShapeDtypeStruct((DEC_BATCH, CONV_W - 1, W_B), F32)],
        scratch_shapes=[pltpu.VMEM((DEC_BATCH, W_B), F32)],
        compiler_params=pltpu.CompilerParams(vmem_limit_bytes=VMEM_LIMIT),
        name="even_mid_sample",
    )(z, state, a_ln_g.reshape(1, W_A), a_ln_b.reshape(1, W_A), ws0, wb0, b_dw, b_dw_bias.reshape(1, W_B),
      b_ln_g.reshape(1, W_B), b_ln_b.reshape(1, W_B))


def _suffix_sum_matrix():
    j = jnp.arange(2 * LANES)[:, None] % LANES
    s = jnp.arange(2 * LANES)[None, :]
    return jnp.where(s < LANES, (j > s).astype(F32), 1.0).astype(BF16)


def _stick_tile(z, tmat, carry, mask):
    t = jnp.log1p(jnp.exp(-jnp.abs(z)))
    log_beta = jnp.minimum(z, 0.0) - t
    log_keep = jnp.minimum(-z, 0.0) - t
    if mask is not None:
        log_keep = jnp.where(mask, log_keep, 0.0)
    hi = log_keep.astype(BF16)
    lo = (log_keep - hi.astype(F32)).astype(BF16)
    cum = _dot(jnp.concatenate([hi, lo], axis=1), tmat)
    w = jnp.exp(log_beta + cum[:, :LANES] + carry)
    if mask is not None:
        w = jnp.where(mask, w, 0.0)
    return w, carry + cum[:, LANES:]


def _attn_prompt_kernel(bias_ref, q_ref, k_ref, v_ref, g_ref, tmat_ref, o_ref, acc_ref, carry_ref, *, tq):
    h = pl.program_id(1)
    qi = pl.program_id(2)
    bias = bias_ref[h]
    scale = HEAD_DIM ** -0.5
    ratio = tq // LANES
    q = q_ref[...]
    acc_ref[...] = jnp.zeros_like(acc_ref)
    carry_ref[...] = jnp.zeros_like(carry_ref)

    def tile(k0, mask):
        kt = k_ref[pl.ds(k0, LANES), :].astype(BF16)
        vt = v_ref[pl.ds(k0, LANES), :].astype(BF16)
        z = _dot_nt(q, kt) * scale + bias
        w, carry = _stick_tile(z, tmat_ref[...], carry_ref[...], mask)
        carry_ref[...] = carry
        acc_ref[...] += _dot(w.astype(BF16), vt)

    row = lax.broadcasted_iota(jnp.int32, (tq, LANES), 0)
    col = lax.broadcasted_iota(jnp.int32, (tq, LANES), 1)
    for r in reversed(range(ratio)):
        tile(pl.multiple_of((qi * ratio + r) * LANES, LANES), col + r * LANES < row)

    def body(n, carry):
        tile(pl.multiple_of((qi * ratio - 1 - n) * LANES, LANES), None)
        return carry

    lax.fori_loop(0, qi * ratio, body, 0)
    o_ref[...] = (acc_ref[...] * _silu(g_ref[...])).astype(o_ref.dtype)


def _attn_prompt(q, k, v, gate, bias, tmat, tq):
    nq = SEQ // tq
    return pl.pallas_call(
        functools.partial(_attn_prompt_kernel, tq=tq),
        grid=(BATCH, N_HEADS_C, nq),
        in_specs=[pl.BlockSpec(memory_space=pltpu.SMEM),
                  pl.BlockSpec((tq, HEAD_DIM), lambda b, h, i: (b * nq + i, h)),
                  pl.BlockSpec((SEQ, HEAD_DIM), lambda b, h, i: (b, h)),
                  pl.BlockSpec((SEQ, HEAD_DIM), lambda b, h, i: (b, h)),
                  pl.BlockSpec((tq, HEAD_DIM), lambda b, h, i: (b * nq + i, h)),
                  pl.BlockSpec((2 * LANES, 2 * LANES), lambda b, h, i: (0, 0))],
        out_specs=pl.BlockSpec((tq, HEAD_DIM), lambda b, h, i: (b * nq + i, h)),
        out_shape=jax.ShapeDtypeStruct((M_PROMPT, W_C), BF16),
        scratch_shapes=[pltpu.VMEM((tq, HEAD_DIM), F32), pltpu.VMEM((tq, LANES), F32)],
        compiler_params=pltpu.CompilerParams(dimension_semantics=("parallel", "parallel", "arbitrary"),
                                             vmem_limit_bytes=VMEM_LIMIT),
        name="attn_prompt",
    )(bias, q, k, v, gate, tmat)


PAGE_ROWS = PAGE_SIZE * N_HEADS_C


def _page_select_matrices():
    pos_of_row = jnp.arange(PAGE_ROWS)[:, None] // N_HEADS_C
    gather = (pos_of_row == jnp.arange(PAGE_SIZE)[None, :]).astype(BF16)
    return gather, gather.T


def _attn_sample_kernel(pt_ref, q_ref, kn_ref, vn_ref, g_ref, bias_ref, tmat_ref, gat_ref, sca_ref, kc_ref, vc_ref,
                        o_ref, acc_ref, carry_ref):
    p = pl.program_id(1)
    scale = HEAD_DIM ** -0.5
    own = (lax.broadcasted_iota(jnp.int32, (N_HEADS_C, PAGE_ROWS), 1) % N_HEADS_C
           == lax.broadcasted_iota(jnp.int32, (N_HEADS_C, PAGE_ROWS), 0))

    @pl.when(p == 0)
    def _():
        lane = lax.broadcasted_iota(jnp.int32, (N_HEADS_C, PAGE_SIZE), 1)
        mask = jnp.logical_and(lane < DEC_SEQ, PAST_LEN + lane < PAST_LEN + DEC_SEQ - 1)
        z = jnp.sum(q_ref[...] * kn_ref[...], axis=1, keepdims=True) * scale + bias_ref[...]
        w, carry = _stick_tile(z, tmat_ref[...], jnp.zeros((N_HEADS_C, LANES), F32), mask)
        carry_ref[...] = carry
        acc_ref[...] = jnp.sum(w, axis=1, keepdims=True) * vn_ref[...]

    @pl.when(p > 0)
    def _():
        k_rows = kc_ref[...].reshape(PAGE_ROWS, HEAD_DIM).astype(BF16)
        v_rows = vc_ref[...].reshape(PAGE_ROWS, HEAD_DIM).astype(BF16)
        z_all = jnp.where(own, _dot_nt(q_ref[...].astype(BF16), k_rows), 0.0)
        hi = z_all.astype(BF16)
        lo = (z_all - hi.astype(F32)).astype(BF16)
        zc = _dot(jnp.concatenate([hi, lo], axis=0), gat_ref[...])
        z = (zc[:N_HEADS_C] + zc[N_HEADS_C:]) * scale + bias_ref[...]
        w, carry = _stick_tile(z, tmat_ref[...], carry_ref[...], None)
        carry_ref[...] = carry
        w_all = jnp.where(own, _dot(w.astype(BF16), sca_ref[...]), 0.0).astype(BF16)
        acc_ref[...] += _dot(w_all, v_rows)

    @pl.when(p == pl.num_programs(1) - 1)
    def _():
        o_ref[...] = (acc_ref[...] * _silu(g_ref[...])).astype(o_ref.dtype)


def _attn_sample(page_table, q, k_new, v_new, gate, bias_rows, tmat, cache_k, cache_v, layer):
    gather, scatter = _page_select_matrices()
    per_b = pl.BlockSpec((None, N_HEADS_C, HEAD_DIM), lambda b, p, pt: (b, 0, 0))
    const = lambda shape: pl.BlockSpec(shape, lambda b, p, pt: (0, 0))
    cache = pl.BlockSpec((None, None, PAGE_SIZE, N_HEADS_C, HEAD_DIM),
                         lambda b, p, pt: (layer, pt[b, N_PAGES - jnp.maximum(p, 1)], 0, 0, 0))
    grid_spec = pltpu.PrefetchScalarGridSpec(
        num_scalar_prefetch=1,
        grid=(DEC_BATCH, N_PAGES + 1),
        in_specs=[per_b, per_b, per_b, per_b, const((N_HEADS_C, LANES)), const((2 * LANES, 2 * LANES)),
                  const((PAGE_ROWS, PAGE_SIZE)), const((PAGE_SIZE, PAGE_ROWS)), cache, cache],
        out_specs=per_b,
        scratch_shapes=[pltpu.VMEM((N_HEADS_C, HEAD_DIM), F32), pltpu.VMEM((N_HEADS_C, LANES), F32)],
    )
    return pl.pallas_call(
        _attn_sample_kernel,
        grid_spec=grid_spec,
        out_shape=jax.ShapeDtypeStruct((DEC_BATCH, N_HEADS_C, HEAD_DIM), BF16),
        compiler_params=pltpu.CompilerParams(dimension_semantics=("parallel", "arbitrary"),
                                             vmem_limit_bytes=VMEM_LIMIT),
        name="attn_sample",
    )(page_table, q, k_new, v_new, gate, bias_rows, tmat, gather, scatter, cache_k, cache_v)


TM = 512
TN = 512
TQ = 256


def _even_layer(xp, xs, state, i, norm_g, w_in, a_ln_g, a_ln_b, a_ws, a_wb, b_dw, b_dw_bias, b_ln_g, b_ln_b, b_pw,
                w_out):
    a_wbt = a_wb[i].T
    ws0 = jnp.repeat(a_ws[i][:, 0, 0], A_GROUP_CH).reshape(1, W_A)
    wb0 = jnp.repeat(a_wb[i][:, 0], A_GROUP_CH).reshape(1, W_A)
    outs = []
    for x, tm, sample in ((xp, TM, False), (xs, DEC_BATCH, True)):
        m = x.shape[0]
        h = _rmsnorm(x, norm_g[i], min(tm, 256))
        z = _matmul([h], w_in, i, [0], 0, EVEN_IN, tm=tm, tn=TN, epilogue=_ep_plain, name="even_in")
        if sample:
            a_out, c, v_n, new_state = _even_mid_sample(z, state[i], a_ln_g[i], a_ln_b[i], ws0, wb0, b_dw[i],
                                                        b_dw_bias[i], b_ln_g[i], b_ln_b[i])
            outs += [v_n, new_state]
        else:
            a_out, c, tail = _even_mid_prompt(z, a_ln_g[i], a_ln_b[i], a_ws[i], a_wbt, b_dw[i], b_dw_bias[i],
                                              b_ln_g[i], b_ln_b[i])
            outs.append(tail)
        gate_cb = (3 * W_A + 2 * W_B) // TN
        b_out = _matmul([c], b_pw, i, [0], 0, W_B, tm=tm, tn=TN, epilogue=_ep_gate, extras=[z],
                        extra_specs=[pl.BlockSpec((tm, TN), lambda j, r: (r, j + gate_cb))], out_dtype=BF16,
                        name="even_pw")
        y = _matmul([a_out, b_out], w_out, i, [0, 1], 0, D_MODEL, tm=tm, tn=TN, epilogue=_ep_resid, extras=[x],
                    extra_specs=[pl.BlockSpec((tm, TN), lambda j, r: (r, j))], name="even_out")
        outs.append(y)
    tail, xp_new, v_n, new_state, xs_new = outs
    return xp_new, xs_new, v_n, tail, new_state


def _odd_layer(xp, xs, cache_k, cache_v, page_table, i, norm_g, w_in, q_norm, k_norm, c_bias, w_out, tmat):
    qg = q_norm[i].reshape(1, HEAD_DIM)
    kg = k_norm[i].reshape(1, HEAD_DIM)
    gspec = pl.BlockSpec((1, HEAD_DIM), lambda j, r: (0, 0))
    res = []
    for x, tm, sample in ((xp, TM, False), (xs, DEC_BATCH, True)):
        h = _rmsnorm(x, norm_g[i], min(tm, 256))
        q = _matmul([h], w_in, i, [0], 0, W_C, tm=tm, tn=TN, epilogue=_ep_headnorm, extras=[qg], extra_specs=[gspec],
                    out_dtype=F32 if sample else BF16, name="odd_q")
        k = _matmul([h], w_in, i, [0], W_C, W_C, tm=tm, tn=TN, epilogue=_ep_headnorm, extras=[kg],
                    extra_specs=[gspec], name="odd_k")
        v = _matmul([h], w_in, i, [0], 2 * W_C, W_C, tm=tm, tn=TN, epilogue=_ep_plain, name="odd_v")
        g = _matmul([h], w_in, i, [0], 3 * W_C, W_C, tm=tm, tn=TN, epilogue=_ep_plain, name="odd_g")
        if sample:
            r3 = lambda a: a.reshape(DEC_BATCH, N_HEADS_C, HEAD_DIM)
            bias_rows = jnp.broadcast_to(c_bias[i].astype(F32)[:, None], (N_HEADS_C, LANES))
            o = _attn_sample(page_table, r3(q), r3(k), r3(v), r3(g), bias_rows, tmat, cache_k, cache_v, i)
            o = o.reshape(DEC_BATCH, W_C)
        else:
            o = _attn_prompt(q, k, v, g, c_bias[i].astype(F32), tmat, TQ)
        y = _matmul([o], w_out, i, [0], 0, D_MODEL, tm=tm, tn=TN, epilogue=_ep_resid, extras=[x],
                    extra_specs=[pl.BlockSpec((tm, TN), lambda j, r: (r, j))], name="odd_out")
        res += [y, k, v]
    return res


def kernel(x_prompt, x_sample, state_conv, cache_k, cache_v, page_table, norm_even, w_in_even, a_ln_g, a_ln_b, a_ws,
           a_wb, b_dw, b_dw_bias, b_ln_g, b_ln_b, b_pw, w_out_even, norm_odd, w_in_odd, q_norm, k_norm, c_bias,
           w_out_odd):
    xp = x_prompt.reshape(M_PROMPT, D_MODEL)
    xs = x_sample.reshape(DEC_BATCH * DEC_SEQ, D_MODEL)
    tmat = _suffix_sum_matrix()
    a_v_s, conv_p, conv_s, k_p, v_p, k_s, v_s = [], [], [], [], [], [], []
    for layer in range(DEPTH):
        i = layer // 2
        if layer % 2 == 0:
            xp, xs, v_n, tail, new_state = _even_layer(xp, xs, state_conv, i, norm_even, w_in_even, a_ln_g, a_ln_b,
                                                       a_ws, a_wb, b_dw, b_dw_bias, b_ln_g, b_ln_b, b_pw, w_out_even)
            a_v_s.append(v_n.reshape(DEC_BATCH, DEC_SEQ, W_A))
            conv_p.append(tail)
            conv_s.append(new_state)
        else:
            xp, kp, vp, xs, kn, vn = _odd_layer(xp, xs, cache_k, cache_v, page_table, i, norm_odd, w_in_odd, q_norm, k_norm,
                                                c_bias, w_out_odd, tmat)
            k_p.append(kp.reshape(BATCH, SEQ, N_HEADS_C, HEAD_DIM))
            v_p.append(vp.reshape(BATCH, SEQ, N_HEADS_C, HEAD_DIM))
            k_s.append(kn.reshape(DEC_BATCH, DEC_SEQ, N_HEADS_C, HEAD_DIM))
            v_s.append(vn.reshape(DEC_BATCH, DEC_SEQ, N_HEADS_C, HEAD_DIM))
    return (xp.reshape(BATCH, SEQ, D_MODEL), xs.reshape(DEC_BATCH, DEC_SEQ, D_MODEL), jnp.stack(a_v_s),
            jnp.stack(conv_p), jnp.stack(conv_s), jnp.stack(k_p), jnp.stack(v_p), jnp.stack(k_s), jnp.stack(v_s))
```

```python
import functools
import math

import jax
import jax.numpy as jnp
from jax import lax
from jax.experimental import pallas as pl
from jax.experimental.pallas import tpu as pltpu

D_MODEL = 4096
BATCH = 4
SEQ = 2048
DEPTH = 4
DEC_BATCH = 8
DEC_SEQ = 1
PAST_LEN = 8192
PAGE_SIZE = 128
N_EVEN = (DEPTH + 1) // 2
N_ODD = DEPTH // 2
W_A = D_MODEL // 2
W_B = D_MODEL // 2
CHUNK = 128
A_GROUP_CH = 128
A_GROUPS = W_A // A_GROUP_CH
CONV_W = 31
W_C = D_MODEL
HEAD_DIM = 128
N_HEADS_C = W_C // HEAD_DIM
EPS = 1e-6
EVEN_IN = 3 * W_A + 3 * W_B
ODD_IN = 4 * W_C
N_PAGES = PAST_LEN // PAGE_SIZE

LANES = 128
SUBLANES = 8
VMEM_LIMIT = 56 * 1024 * 1024

F32 = jnp.float32
BF16 = jnp.bfloat16

M_PROMPT = BATCH * SEQ
CONV_PAD = 32

assert DEC_SEQ == 1 and CHUNK == LANES and HEAD_DIM == LANES and A_GROUP_CH == LANES


def _gelu(x):
    return 0.5 * x * (1.0 + jnp.tanh(math.sqrt(2.0 / math.pi) * (x + 0.044715 * (x * x * x))))


def _sigmoid(x):
    return 1.0 / (1.0 + jnp.exp(-x))


def _silu(x):
    return x * _sigmoid(x)


def _layernorm(x, g, b):
    xc = x - jnp.mean(x, axis=-1, keepdims=True)
    y = xc * lax.rsqrt(jnp.mean(xc * xc, axis=-1, keepdims=True) + EPS)
    return y * g + b


def _dot(a, b):
    return jnp.dot(a, b, preferred_element_type=F32)


def _dot_nt(a, b):
    return lax.dot_general(a, b, (((1,), (1,)), ((), ())), preferred_element_type=F32)


def _rmsnorm_kernel(x_ref, g_ref, o_ref):
    x = x_ref[...]
    y = x * lax.rsqrt(jnp.mean(x * x, axis=-1, keepdims=True) + EPS)
    o_ref[...] = (y * g_ref[...]).astype(o_ref.dtype)


def _rmsnorm(x, g, tm):
    m, d = x.shape
    return pl.pallas_call(
        _rmsnorm_kernel,
        grid=(m // tm,),
        in_specs=[pl.BlockSpec((tm, d), lambda i: (i, 0)), pl.BlockSpec((1, d), lambda i: (0, 0))],
        out_specs=pl.BlockSpec((tm, d), lambda i: (i, 0)),
        out_shape=jax.ShapeDtypeStruct((m, d), BF16),
        compiler_params=pltpu.CompilerParams(dimension_semantics=("parallel",), vmem_limit_bytes=VMEM_LIMIT),
        name="rmsnorm",
    )(x, g.reshape(1, d))


def _mm_kernel(*refs, n_x, n_extra, epilogue):
    x_refs = refs[:n_x]
    w_refs = refs[n_x:2 * n_x]
    extra_refs = refs[2 * n_x:2 * n_x + n_extra]
    o_ref = refs[2 * n_x + n_extra]
    wbf_refs = refs[2 * n_x + n_extra + 1:]

    @pl.when(pl.program_id(1) == 0)
    def _():
        for w_ref, wbf_ref in zip(w_refs, wbf_refs):
            wbf_ref[...] = w_ref[...].astype(BF16)

    acc = _dot(x_refs[0][...], wbf_refs[0][...])
    for x_ref, wbf_ref in zip(x_refs[1:], wbf_refs[1:]):
        acc = acc + _dot(x_ref[...], wbf_ref[...])
    epilogue(acc, o_ref, *extra_refs)


def _matmul(xs, w, layer, row_blocks, col_off, n, *, tm, tn, epilogue, extras=(), extra_specs=(), out_dtype=F32,
            name="matmul"):
    m = xs[0].shape[0]
    cb = col_off // tn
    in_specs = [pl.BlockSpec((tm, x.shape[1]), lambda j, i: (i, 0)) for x in xs]
    for x, rb in zip(xs, row_blocks):
        in_specs.append(pl.BlockSpec((None, x.shape[1], tn), functools.partial(lambda j, i, rb: (layer, rb, j + cb), rb=rb)))
    in_specs += list(extra_specs)
    kern = functools.partial(_mm_kernel, n_x=len(xs), n_extra=len(extras), epilogue=epilogue)
    return pl.pallas_call(
        kern,
        grid=(n // tn, m // tm),
        in_specs=in_specs,
        out_specs=pl.BlockSpec((tm, tn), lambda j, i: (i, j)),
        out_shape=jax.ShapeDtypeStruct((m, n), out_dtype),
        scratch_shapes=[pltpu.VMEM((x.shape[1], tn), BF16) for x in xs],
        compiler_params=pltpu.CompilerParams(dimension_semantics=("parallel", "arbitrary"),
                                             vmem_limit_bytes=VMEM_LIMIT),
        name=name,
    )(*xs, *([w] * len(xs)), *extras)


def _ep_plain(acc, o_ref):
    o_ref[...] = acc.astype(o_ref.dtype)


def _ep_headnorm(acc, o_ref, g_ref, *, mult):
    g = g_ref[...] * mult
    for h in range(acc.shape[1] // HEAD_DIM):
        a = acc[:, h * HEAD_DIM:(h + 1) * HEAD_DIM]
        y = a * lax.rsqrt(jnp.mean(a * a, axis=-1, keepdims=True) + EPS)
        o_ref[:, h * HEAD_DIM:(h + 1) * HEAD_DIM] = (y * g).astype(o_ref.dtype)


def _ep_gate(acc, o_ref, gate_ref):
    o_ref[...] = (acc * _silu(gate_ref[...])).astype(o_ref.dtype)


def _ep_resid(acc, o_ref, r_ref):
    o_ref[...] = (r_ref[...] + acc).astype(o_ref.dtype)


def _even_mid_kernel(au_ref, av_ref, ag_ref, ba_ref, bb_ref, alg_ref, alb_ref, ws_ref, wbt_ref, dw_ref, dwb_ref,
                     blg_ref, blb_ref, aout_ref, c_ref, tail_ref, wsb_ref, xp_ref, conv_ref):
    b = pl.program_id(0)
    c = pl.program_id(1)
    n_chunks = pl.num_programs(1)

    @pl.when(jnp.logical_and(b == 0, c == 0))
    def _():
        t_idx = lax.broadcasted_iota(jnp.int32, (CHUNK, CHUNK), 0)
        s_idx = lax.broadcasted_iota(jnp.int32, (CHUNK, CHUNK), 1)
        for g in range(A_GROUPS):
            wsb_ref[g] = jnp.where(s_idx <= t_idx, ws_ref[g], 0.0).astype(BF16)

    v_n = _layernorm(_gelu(av_ref[...]), alg_ref[...], alb_ref[...])
    v_b = v_n.astype(BF16)
    for g in range(A_GROUPS):
        sl = slice(g * A_GROUP_CH, (g + 1) * A_GROUP_CH)
        sp = _dot(wsb_ref[g], v_b[:, sl]) + wbt_ref[:, g:g + 1]
        aout_ref[:, sl] = (_gelu(au_ref[:, sl]) * sp * _silu(ag_ref[:, sl])).astype(aout_ref.dtype)

    @pl.when(c == 0)
    def _():
        xp_ref[0:CONV_PAD, :] = jnp.zeros((CONV_PAD, W_B), F32)

    xp_ref[CONV_PAD:CONV_PAD + CHUNK, :] = ba_ref[...] * _sigmoid(bb_ref[...])
    first = CONV_PAD - (CONV_W - 1)
    for cb in range(W_B // LANES):
        sl = slice(cb * LANES, (cb + 1) * LANES)
        conv = jnp.broadcast_to(dwb_ref[:, sl], (CHUNK, LANES))
        for r in range(SUBLANES):
            n_rows = CHUNK + (SUBLANES if r else 0)
            part = None
            for j in range(CONV_W):
                if (first + j) % SUBLANES == r:
                    a0 = first + j - r
                    term = xp_ref[a0:a0 + n_rows, sl] * dw_ref[j:j + 1, sl]
                    part = term if part is None else part + term
            conv = conv + part[r:r + CHUNK]
        conv_ref[:, sl] = conv
    c_ref[...] = _silu(_layernorm(conv_ref[...], blg_ref[...], blb_ref[...])).astype(c_ref.dtype)

    @pl.when(c == n_chunks - 1)
    def _():
        tail_ref[...] = xp_ref[CONV_PAD + CHUNK - (CONV_W - 1):CONV_PAD + CHUNK, :]

    xp_ref[0:CONV_PAD, :] = xp_ref[CHUNK:CHUNK + CONV_PAD, :]


def _even_mid_prompt(z, a_ln_g, a_ln_b, a_ws, a_wbt, b_dw, b_dw_bias, b_ln_g, b_ln_b):
    n_chunks = SEQ // CHUNK
    zspec = lambda k: pl.BlockSpec((CHUNK, W_A), functools.partial(lambda b, c, k: (b * n_chunks + c, k), k=k))
    row = lambda w: pl.BlockSpec((1, w), lambda b, c: (0, 0))
    return pl.pallas_call(
        _even_mid_kernel,
        grid=(BATCH, n_chunks),
        in_specs=[zspec(0), zspec(1), zspec(2), zspec(3), zspec(4), row(W_A), row(W_A),
                  pl.BlockSpec((A_GROUPS, CHUNK, CHUNK), lambda b, c: (0, 0, 0)),
                  pl.BlockSpec((CHUNK, A_GROUPS), lambda b, c: (0, 0)),
                  pl.BlockSpec((CONV_W, W_B), lambda b, c: (0, 0)), row(W_B), row(W_B), row(W_B)],
        out_specs=[pl.BlockSpec((CHUNK, W_A), lambda b, c: (b * n_chunks + c, 0)),
                   pl.BlockSpec((CHUNK, W_B), lambda b, c: (b * n_chunks + c, 0)),
                   pl.BlockSpec((None, CONV_W - 1, W_B), lambda b, c: (b, 0, 0))],
        out_shape=[jax.ShapeDtypeStruct((M_PROMPT, W_A), BF16), jax.ShapeDtypeStruct((M_PROMPT, W_B), BF16),
                   jax.ShapeDtypeStruct((BATCH, CONV_W - 1, W_B), F32)],
        scratch_shapes=[pltpu.VMEM((A_GROUPS, CHUNK, CHUNK), BF16), pltpu.VMEM((CONV_PAD + CHUNK, W_B), F32),
                        pltpu.VMEM((CHUNK, W_B), F32)],
        compiler_params=pltpu.CompilerParams(dimension_semantics=("arbitrary", "arbitrary"),
                                             vmem_limit_bytes=VMEM_LIMIT),
        name="even_mid",
    )(z, z, z, z, z, a_ln_g.reshape(1, W_A), a_ln_b.reshape(1, W_A), a_ws, a_wbt, b_dw, b_dw_bias.reshape(1, W_B),
      b_ln_g.reshape(1, W_B), b_ln_b.reshape(1, W_B))


def _even_mid_sample_kernel(z_ref, st_ref, alg_ref, alb_ref, ws0_ref, wb0_ref, dw_ref, dwb_ref, blg_ref, blb_ref,
                            aout_ref, c_ref, vn_ref, nst_ref, conv_ref):
    a_u = z_ref[:, 0:W_A]
    a_v = z_ref[:, W_A:2 * W_A]
    a_g = z_ref[:, 2 * W_A:3 * W_A]
    b_a = z_ref[:, 3 * W_A:3 * W_A + W_B]
    b_b = z_ref[:, 3 * W_A + W_B:3 * W_A + 2 * W_B]
    v_n = _layernorm(_gelu(a_v), alg_ref[...], alb_ref[...])
    vn_ref[...] = v_n
    sp = ws0_ref[...] * v_n + wb0_ref[...]
    aout_ref[...] = (_gelu(a_u) * sp * _silu(a_g)).astype(aout_ref.dtype)
    glu = b_a * _sigmoid(b_b)
    hist = CONV_W - 1
    for b in range(DEC_BATCH):
        st = st_ref[b]
        row = glu[b:b + 1, :]
        conv_ref[b:b + 1, :] = (jnp.sum(st * dw_ref[0:hist, :], axis=0, keepdims=True) + row * dw_ref[hist:CONV_W, :]
                                + dwb_ref[...])
        nst_ref[b, 0:hist - 1, :] = st[1:hist, :]
        nst_ref[b, hist - 1:hist, :] = row
    c_ref[...] = _silu(_layernorm(conv_ref[...], blg_ref[...], blb_ref[...])).astype(c_ref.dtype)


def _even_mid_sample(z, state, a_ln_g, a_ln_b, ws0, wb0, b_dw, b_dw_bias, b_ln_g, b_ln_b):
    return pl.pallas_call(
        _even_mid_sample_kernel,
        out_shape=[jax.ShapeDtypeStruct((DEC_BATCH, W_A), BF16), jax.ShapeDtypeStruct((DEC_BATCH, W_B), BF16),
                   jax.ShapeDtypeStruct((DEC_BATCH, W_A), F32),
                   jax.ShapeDtypeStruct((DEC_BATCH, CONV_W - 1, W_B), F32)],
        scratch_shapes=[pltpu.VMEM((DEC_BATCH, W_B), F32)],
        compiler_params=pltpu.CompilerParams(vmem_limit_bytes=VMEM_LIMIT),
        name="even_mid_sample",
    )(z, state, a_ln_g.reshape(1, W_A), a_ln_b.reshape(1, W_A), ws0, wb0, b_dw, b_dw_bias.reshape(1, W_B),
      b_ln_g.reshape(1, W_B), b_ln_b.reshape(1, W_B))


SCALE = HEAD_DIM ** -0.5
LOG2E = math.log2(math.e)


def _suffix_sum_matrix():
    j = jnp.arange(2 * LANES)[:, None] % LANES
    s = jnp.arange(2 * LANES)[None, :]
    return jnp.where(s < LANES, -(j >= s).astype(F32), -1.0).astype(BF16)


def _stick_sums(z, tmat, masks):
    e = jnp.exp2(jnp.abs(z) * (-LOG2E))
    sp = jnp.maximum(z, 0.0) + jnp.log(1.0 + e)
    cums = []
    for s in range(z.shape[1] // LANES):
        sp_s = sp[:, s * LANES:(s + 1) * LANES]
        if masks[s] is not None:
            sp_s = jnp.where(masks[s], sp_s, 0.0)
        hi = sp_s.astype(BF16)
        lo = (sp_s - hi.astype(F32)).astype(BF16)
        cums.append(_dot(jnp.concatenate([hi, lo], axis=1), tmat))
    return cums


def _stick_weights(z, cums, carry, masks):
    n_sub = len(cums)
    ws = [None] * n_sub
    for s in reversed(range(n_sub)):
        w_s = jnp.exp(z[:, s * LANES:(s + 1) * LANES] + cums[s][:, :LANES] + carry)
        if masks[s] is not None:
            w_s = jnp.where(masks[s], w_s, 0.0)
        ws[s] = w_s.astype(BF16)
        carry = carry + cums[s][:, LANES:]
    return ws, carry


def _weighted_values(acc, ws, v):
    return acc + _dot(ws[0] if len(ws) == 1 else jnp.concatenate(ws, axis=1), v)


def _attn_prompt_kernel(bias_ref, q_ref, k_ref, v_ref, g_ref, tmat_ref, o_ref, acc_ref, carry_ref, *, tq):
    h = pl.program_id(1)
    qi = pl.program_id(2)
    bias = bias_ref[h]
    half = tq // 2
    acc_ref[...] = jnp.zeros_like(acc_ref)
    carry_ref[...] = jnp.zeros_like(carry_ref)

    def blocks(specs):
        zs, cums, outs = [], [], []
        for r0, n_rows, k0, n_keys, masks in specs:
            kt = k_ref[pl.ds(k0, n_keys), :].astype(BF16)
            zs.append(_dot_nt(q_ref[r0:r0 + n_rows, :], kt) + bias)
        for (r0, n_rows, k0, n_keys, masks), z in zip(specs, zs):
            cums.append(_stick_sums(z, tmat_ref[...], masks))
        for (r0, n_rows, k0, n_keys, masks), z, cum in zip(specs, zs, cums):
            rows = slice(r0, r0 + n_rows)
            ws, carry = _stick_weights(z, cum, carry_ref[rows, :], masks)
            vt = v_ref[pl.ds(k0, n_keys), :].astype(BF16)
            outs.append((rows, carry, _weighted_values(acc_ref[rows, :], ws, vt)))
        for rows, carry, acc in outs:
            carry_ref[rows, :] = carry
            acc_ref[rows, :] = acc

    row = lax.broadcasted_iota(jnp.int32, (half, LANES), 0)
    col = lax.broadcasted_iota(jnp.int32, (half, LANES), 1)
    near = col < row
    far = col + LANES < row
    q0 = pl.multiple_of(qi * tq, tq)
    blocks([(0, half, q0, half, [near, far]), (half, half, q0, tq, [None, None, near, far])])

    def body(n, carry):
        k0 = pl.multiple_of((qi - 1 - n) * tq, tq)
        blocks([(0, tq, k0, tq, [None] * (tq // LANES))])
        return carry

    lax.fori_loop(0, qi, body, 0)
    o_ref[...] = (acc_ref[...] * _silu(g_ref[...])).astype(o_ref.dtype)


def _attn_prompt(q, k, v, gate, bias, tmat, tq):
    assert tq == 4 * LANES
    nq = SEQ // tq
    return pl.pallas_call(
        functools.partial(_attn_prompt_kernel, tq=tq),
        grid=(BATCH, N_HEADS_C, nq),
        in_specs=[pl.BlockSpec(memory_space=pltpu.SMEM),
                  pl.BlockSpec((tq, HEAD_DIM), lambda b, h, i: (b * nq + i, h)),
                  pl.BlockSpec((SEQ, HEAD_DIM), lambda b, h, i: (b, h)),
                  pl.BlockSpec((SEQ, HEAD_DIM), lambda b, h, i: (b, h)),
                  pl.BlockSpec((tq, HEAD_DIM), lambda b, h, i: (b * nq + i, h)),
                  pl.BlockSpec((2 * LANES, 2 * LANES), lambda b, h, i: (0, 0))],
        out_specs=pl.BlockSpec((tq, HEAD_DIM), lambda b, h, i: (b * nq + i, h)),
        out_shape=jax.ShapeDtypeStruct((M_PROMPT, W_C), BF16),
        scratch_shapes=[pltpu.VMEM((tq, HEAD_DIM), F32), pltpu.VMEM((tq, LANES), F32)],
        compiler_params=pltpu.CompilerParams(dimension_semantics=("parallel", "parallel", "arbitrary"),
                                             vmem_limit_bytes=VMEM_LIMIT),
        name="attn_prompt",
    )(bias, q, k, v, gate, tmat)


PAGE_ROWS = PAGE_SIZE * N_HEADS_C


def _page_select_matrices():
    pos_of_row = jnp.arange(PAGE_ROWS)[:, None] // N_HEADS_C
    gather = (pos_of_row == jnp.arange(PAGE_SIZE)[None, :]).astype(BF16)
    return gather, gather.T


PAGES_PER_STEP = 2


def _attn_sample_kernel(pt_ref, q_ref, kn_ref, vn_ref, g_ref, bias_ref, tmat_ref, gat_ref, sca_ref, *refs):
    kc_refs = refs[:PAGES_PER_STEP]
    vc_refs = refs[PAGES_PER_STEP:2 * PAGES_PER_STEP]
    o_ref, acc_ref, carry_ref = refs[2 * PAGES_PER_STEP:]
    p = pl.program_id(1)
    bias = bias_ref[...]
    own = (lax.broadcasted_iota(jnp.int32, (N_HEADS_C, PAGE_ROWS), 1) % N_HEADS_C
           == lax.broadcasted_iota(jnp.int32, (N_HEADS_C, PAGE_ROWS), 0))

    @pl.when(p == 0)
    def _():
        lane = lax.broadcasted_iota(jnp.int32, (N_HEADS_C, PAGE_SIZE), 1)
        mask = jnp.logical_and(lane < DEC_SEQ, PAST_LEN + lane < PAST_LEN + DEC_SEQ - 1)
        z = jnp.sum(q_ref[...] * kn_ref[...], axis=1, keepdims=True) + bias
        ws, carry = _stick_weights(z, _stick_sums(z, tmat_ref[...], [mask]), jnp.zeros((N_HEADS_C, LANES), F32),
                                   [mask])
        carry_ref[...] = carry
        acc_ref[...] = jnp.sum(ws[0].astype(F32), axis=1, keepdims=True) * vn_ref[...]

    @pl.when(p > 0)
    def _():
        qb = q_ref[...].astype(BF16)
        zs, cums = [], []
        for kc_ref in kc_refs:
            k_rows = kc_ref[...].reshape(PAGE_ROWS, HEAD_DIM).astype(BF16)
            z_all = jnp.where(own, _dot_nt(qb, k_rows), 0.0)
            hi = z_all.astype(BF16)
            lo = (z_all - hi.astype(F32)).astype(BF16)
            zc = _dot(jnp.concatenate([hi, lo], axis=0), gat_ref[...])
            zs.append(zc[:N_HEADS_C] + zc[N_HEADS_C:] + bias)
        for z in zs:
            cums.append(_stick_sums(z, tmat_ref[...], [None]))
        carry = carry_ref[...]
        acc = acc_ref[...]
        for z, cum, vc_ref in zip(zs, cums, vc_refs):
            ws, carry = _stick_weights(z, cum, carry, [None])
            w_all = jnp.where(own, _dot(ws[0], sca_ref[...]), 0.0).astype(BF16)
            acc = acc + _dot(w_all, vc_ref[...].reshape(PAGE_ROWS, HEAD_DIM).astype(BF16))
        carry_ref[...] = carry
        acc_ref[...] = acc

    @pl.when(p == pl.num_programs(1) - 1)
    def _():
        o_ref[...] = (acc_ref[...] * _silu(g_ref[...])).astype(o_ref.dtype)


def _attn_sample(page_table, q, k_new, v_new, gate, bias_rows, tmat, cache_k, cache_v, layer):
    gather, scatter = _page_select_matrices()
    per_b = pl.BlockSpec((None, N_HEADS_C, HEAD_DIM), lambda b, p, pt: (b, 0, 0))
    const = lambda shape: pl.BlockSpec(shape, lambda b, p, pt: (0, 0))
    assert N_PAGES % PAGES_PER_STEP == 0

    def cache(r):
        return pl.BlockSpec(
            (None, None, PAGE_SIZE, N_HEADS_C, HEAD_DIM),
            lambda b, p, pt: (layer, pt[b, N_PAGES - 1 - r - PAGES_PER_STEP * (jnp.maximum(p, 1) - 1)], 0, 0, 0))

    caches = [cache(r) for r in range(PAGES_PER_STEP)]
    grid_spec = pltpu.PrefetchScalarGridSpec(
        num_scalar_prefetch=1,
        grid=(DEC_BATCH, N_PAGES // PAGES_PER_STEP + 1),
        in_specs=[per_b, per_b, per_b, per_b, const((N_HEADS_C, LANES)), const((2 * LANES, 2 * LANES)),
                  const((PAGE_ROWS, PAGE_SIZE)), const((PAGE_SIZE, PAGE_ROWS))] + caches + caches,
        out_specs=per_b,
        scratch_shapes=[pltpu.VMEM((N_HEADS_C, HEAD_DIM), F32), pltpu.VMEM((N_HEADS_C, LANES), F32)],
    )
    return pl.pallas_call(
        _attn_sample_kernel,
        grid_spec=grid_spec,
        out_shape=jax.ShapeDtypeStruct((DEC_BATCH, N_HEADS_C, HEAD_DIM), BF16),
        compiler_params=pltpu.CompilerParams(dimension_semantics=("parallel", "arbitrary"),
                                             vmem_limit_bytes=VMEM_LIMIT),
        name="attn_sample",
    )(page_table, q, k_new, v_new, gate, bias_rows, tmat, gather, scatter, *([cache_k] * PAGES_PER_STEP),
      *([cache_v] * PAGES_PER_STEP))


TM = 1024
TN = 512
TQ = 512


def _even_layer(xp, xs, state, i, norm_g, w_in, a_ln_g, a_ln_b, a_ws, a_wb, b_dw, b_dw_bias, b_ln_g, b_ln_b, b_pw,
                w_out):
    a_wbt = a_wb[i].T
    ws0 = jnp.repeat(a_ws[i][:, 0, 0], A_GROUP_CH).reshape(1, W_A)
    wb0 = jnp.repeat(a_wb[i][:, 0], A_GROUP_CH).reshape(1, W_A)
    outs = []
    for x, tm, sample in ((xp, TM, False), (xs, DEC_BATCH, True)):
        m = x.shape[0]
        h = _rmsnorm(x, norm_g[i], min(tm, 256))
        z = _matmul([h], w_in, i, [0], 0, EVEN_IN, tm=tm, tn=TN, epilogue=_ep_plain, name="even_in")
        if sample:
            a_out, c, v_n, new_state = _even_mid_sample(z, state[i], a_ln_g[i], a_ln_b[i], ws0, wb0, b_dw[i],
                                                        b_dw_bias[i], b_ln_g[i], b_ln_b[i])
            outs += [v_n, new_state]
        else:
            a_out, c, tail = _even_mid_prompt(z, a_ln_g[i], a_ln_b[i], a_ws[i], a_wbt, b_dw[i], b_dw_bias[i],
                                              b_ln_g[i], b_ln_b[i])
            outs.append(tail)
        gate_cb = (3 * W_A + 2 * W_B) // TN
        b_out = _matmul([c], b_pw, i, [0], 0, W_B, tm=tm, tn=TN, epilogue=_ep_gate, extras=[z],
                        extra_specs=[pl.BlockSpec((tm, TN), lambda j, r: (r, j + gate_cb))], out_dtype=BF16,
                        name="even_pw")
        y = _matmul([a_out, b_out], w_out, i, [0, 1], 0, D_MODEL, tm=tm, tn=TN, epilogue=_ep_resid, extras=[x],
                    extra_specs=[pl.BlockSpec((tm, TN), lambda j, r: (r, j))], name="even_out")
        outs.append(y)
    tail, xp_new, v_n, new_state, xs_new = outs
    return xp_new, xs_new, v_n, tail, new_state


def _odd_layer(xp, xs, cache_k, cache_v, page_table, i, norm_g, w_in, q_norm, k_norm, c_bias, w_out, tmat):
    qg = q_norm[i].reshape(1, HEAD_DIM)
    kg = k_norm[i].reshape(1, HEAD_DIM)
    gspec = pl.BlockSpec((1, HEAD_DIM), lambda j, r: (0, 0))
    res = []
    for x, tm, sample in ((xp, TM, False), (xs, DEC_BATCH, True)):
        h = _rmsnorm(x, norm_g[i], min(tm, 256))
        q = _matmul([h], w_in, i, [0], 0, W_C, tm=tm, tn=TN, epilogue=functools.partial(_ep_headnorm, mult=SCALE),
                    extras=[qg], extra_specs=[gspec], out_dtype=F32 if sample else BF16, name="odd_q")
        k = _matmul([h], w_in, i, [0], W_C, W_C, tm=tm, tn=TN, epilogue=functools.partial(_ep_headnorm, mult=1.0),
                    extras=[kg], extra_specs=[gspec], name="odd_k")
        v = _matmul([h], w_in, i, [0], 2 * W_C, W_C, tm=tm, tn=TN, epilogue=_ep_plain, name="odd_v")
        g = _matmul([h], w_in, i, [0], 3 * W_C, W_C, tm=tm, tn=TN, epilogue=_ep_plain, name="odd_g")
        if sample:
            r3 = lambda a: a.reshape(DEC_BATCH, N_HEADS_C, HEAD_DIM)
            bias_rows = jnp.broadcast_to(c_bias[i].astype(F32)[:, None], (N_HEADS_C, LANES))
            o = _attn_sample(page_table, r3(q), r3(k), r3(v), r3(g), bias_rows, tmat, cache_k, cache_v, i)
            o = o.reshape(DEC_BATCH, W_C)
        else:
            o = _attn_prompt(q, k, v, g, c_bias[i].astype(F32), tmat, TQ)
        y = _matmul([o], w_out, i, [0], 0, D_MODEL, tm=tm, tn=TN, epilogue=_ep_resid, extras=[x],
                    extra_specs=[pl.BlockSpec((tm, TN), lambda j, r: (r, j))], name="odd_out")
        res += [y, k, v]
    return res


def kernel(x_prompt, x_sample, state_conv, cache_k, cache_v, page_table, norm_even, w_in_even, a_ln_g, a_ln_b, a_ws,
           a_wb, b_dw, b_dw_bias, b_ln_g, b_ln_b, b_pw, w_out_even, norm_odd, w_in_odd, q_norm, k_norm, c_bias,
           w_out_odd):
    xp = x_prompt.reshape(M_PROMPT, D_MODEL)
    xs = x_sample.reshape(DEC_BATCH * DEC_SEQ, D_MODEL)
    tmat = _suffix_sum_matrix()
    a_v_s, conv_p, conv_s, k_p, v_p, k_s, v_s = [], [], [], [], [], [], []
    for layer in range(DEPTH):
        i = layer // 2
        if layer % 2 == 0:
            xp, xs, v_n, tail, new_state = _even_layer(xp, xs, state_conv, i, norm_even, w_in_even, a_ln_g, a_ln_b,
                                                       a_ws, a_wb, b_dw, b_dw_bias, b_ln_g, b_ln_b, b_pw, w_out_even)
            a_v_s.append(v_n.reshape(DEC_BATCH, DEC_SEQ, W_A))
            conv_p.append(tail)
            conv_s.append(new_state)
        else:
            xp, kp, vp, xs, kn, vn = _odd_layer(xp, xs, cache_k, cache_v, page_table, i, norm_odd, w_in_odd, q_norm, k_norm,
                                                c_bias, w_out_odd, tmat)
            k_p.append(kp.reshape(BATCH, SEQ, N_HEADS_C, HEAD_DIM))
            v_p.append(vp.reshape(BATCH, SEQ, N_HEADS_C, HEAD_DIM))
            k_s.append(kn.reshape(DEC_BATCH, DEC_SEQ, N_HEADS_C, HEAD_DIM))
            v_s.append(vn.reshape(DEC_BATCH, DEC_SEQ, N_HEADS_C, HEAD_DIM))
    return (xp.reshape(BATCH, SEQ, D_MODEL), xs.reshape(DEC_BATCH, DEC_SEQ, D_MODEL), jnp.stack(a_v_s),
            jnp.stack(conv_p), jnp.stack(conv_s), jnp.stack(k_p), jnp.stack(v_p), jnp.stack(k_s), jnp.stack(v_s))
```

```python
import functools
import math

import jax
import jax.numpy as jnp
from jax import lax
from jax.experimental import pallas as pl
from jax.experimental.pallas import tpu as pltpu

D_MODEL = 4096
BATCH = 4
SEQ = 2048
DEPTH = 4
DEC_BATCH = 8
DEC_SEQ = 1
PAST_LEN = 8192
PAGE_SIZE = 128
N_EVEN = (DEPTH + 1) // 2
N_ODD = DEPTH // 2
W_A = D_MODEL // 2
W_B = D_MODEL // 2
CHUNK = 128
A_GROUP_CH = 128
A_GROUPS = W_A // A_GROUP_CH
CONV_W = 31
W_C = D_MODEL
HEAD_DIM = 128
N_HEADS_C = W_C // HEAD_DIM
EPS = 1e-6
EVEN_IN = 3 * W_A + 3 * W_B
ODD_IN = 4 * W_C
N_PAGES = PAST_LEN // PAGE_SIZE

LANES = 128
SUBLANES = 8
VMEM_LIMIT = 56 * 1024 * 1024

F32 = jnp.float32
BF16 = jnp.bfloat16

M_PROMPT = BATCH * SEQ
CONV_PAD = 32

assert DEC_SEQ == 1 and CHUNK == LANES and HEAD_DIM == LANES and A_GROUP_CH == LANES


def _gelu(x):
    return 0.5 * x * (1.0 + jnp.tanh(math.sqrt(2.0 / math.pi) * (x + 0.044715 * (x * x * x))))


def _sigmoid(x):
    return 1.0 / (1.0 + jnp.exp(-x))


def _silu(x):
    return x * _sigmoid(x)


def _layernorm(x, g, b):
    xc = x - jnp.mean(x, axis=-1, keepdims=True)
    y = xc * lax.rsqrt(jnp.mean(xc * xc, axis=-1, keepdims=True) + EPS)
    return y * g + b


def _dot(a, b):
    return jnp.dot(a, b, preferred_element_type=F32)


def _dot_nt(a, b):
    return lax.dot_general(a, b, (((1,), (1,)), ((), ())), preferred_element_type=F32)


def _rmsnorm_kernel(x_ref, g_ref, o_ref):
    x = x_ref[...]
    y = x * lax.rsqrt(jnp.mean(x * x, axis=-1, keepdims=True) + EPS)
    o_ref[...] = (y * g_ref[...]).astype(o_ref.dtype)


def _rmsnorm(x, g, tm):
    m, d = x.shape
    return pl.pallas_call(
        _rmsnorm_kernel,
        grid=(m // tm,),
        in_specs=[pl.BlockSpec((tm, d), lambda i: (i, 0)), pl.BlockSpec((1, d), lambda i: (0, 0))],
        out_specs=pl.BlockSpec((tm, d), lambda i: (i, 0)),
        out_shape=jax.ShapeDtypeStruct((m, d), BF16),
        compiler_params=pltpu.CompilerParams(dimension_semantics=("parallel",), vmem_limit_bytes=VMEM_LIMIT),
        name="rmsnorm",
    )(x, g.reshape(1, d))


def _mm_kernel(*refs, n_x, n_extra, epilogue):
    x_refs, xs_refs, w_refs = refs[:n_x], refs[n_x:2 * n_x], refs[2 * n_x:3 * n_x]
    extra_refs = refs[3 * n_x:3 * n_x + n_extra]
    extra_s_refs = refs[3 * n_x + n_extra:3 * n_x + 2 * n_extra]
    o_ref, os_ref = refs[3 * n_x + 2 * n_extra:3 * n_x + 2 * n_extra + 2]
    wbf_refs = refs[3 * n_x + 2 * n_extra + 2:]

    def product(rows):
        acc = _dot(rows[0][...], wbf_refs[0][...])
        for x_ref, wbf_ref in zip(rows[1:], wbf_refs[1:]):
            acc = acc + _dot(x_ref[...], wbf_ref[...])
        return acc

    @pl.when(pl.program_id(1) == 0)
    def _():
        for w_ref, wbf_ref in zip(w_refs, wbf_refs):
            wbf_ref[...] = w_ref[...].astype(BF16)
        epilogue(product(xs_refs), os_ref, *extra_s_refs)

    epilogue(product(x_refs), o_ref, *extra_refs)


def _matmul(xs, xs_s, w, layer, row_blocks, col_off, n, *, tm, tn, epilogue, row_extras=(), params=(),
            out_dtype=F32, name="matmul"):
    m, ms = xs[0].shape[0], xs_s[0].shape[0]
    cb = col_off // tn
    in_specs = [pl.BlockSpec((tm, x.shape[1]), lambda j, i: (i, 0)) for x in xs]
    in_specs += [pl.BlockSpec((ms, x.shape[1]), lambda j, i: (0, 0)) for x in xs_s]
    for x, rb in zip(xs, row_blocks):
        in_specs.append(pl.BlockSpec((None, x.shape[1], tn), functools.partial(lambda j, i, rb: (layer, rb, j + cb), rb=rb)))
    param_specs = [pl.BlockSpec(p.shape, lambda j, i: (0, 0)) for p in params]
    in_specs += [pl.BlockSpec((tm, tn), functools.partial(lambda j, i, c: (i, j + c), c=c0 // tn))
                 for _, _, c0 in row_extras] + param_specs
    in_specs += [pl.BlockSpec((ms, tn), functools.partial(lambda j, i, c: (0, j + c), c=c0 // tn))
                 for _, _, c0 in row_extras] + param_specs
    kern = functools.partial(_mm_kernel, n_x=len(xs), n_extra=len(row_extras) + len(params), epilogue=epilogue)
    return pl.pallas_call(
        kern,
        grid=(n // tn, m // tm),
        in_specs=in_specs,
        out_specs=[pl.BlockSpec((tm, tn), lambda j, i: (i, j)), pl.BlockSpec((ms, tn), lambda j, i: (0, j))],
        out_shape=[jax.ShapeDtypeStruct((m, n), out_dtype), jax.ShapeDtypeStruct((ms, n), out_dtype)],
        scratch_shapes=[pltpu.VMEM((x.shape[1], tn), BF16) for x in xs],
        compiler_params=pltpu.CompilerParams(dimension_semantics=("parallel", "arbitrary"),
                                             vmem_limit_bytes=VMEM_LIMIT),
        name=name,
    )(*xs, *xs_s, *([w] * len(xs)), *[a for a, _, _ in row_extras], *params, *[a for _, a, _ in row_extras],
      *params)


def _ep_plain(acc, o_ref):
    o_ref[...] = acc.astype(o_ref.dtype)


def _ep_headnorm(acc, o_ref, g_ref, *, mult):
    g = g_ref[...] * mult
    for h in range(acc.shape[1] // HEAD_DIM):
        a = acc[:, h * HEAD_DIM:(h + 1) * HEAD_DIM]
        y = a * lax.rsqrt(jnp.mean(a * a, axis=-1, keepdims=True) + EPS)
        o_ref[:, h * HEAD_DIM:(h + 1) * HEAD_DIM] = (y * g).astype(o_ref.dtype)


def _ep_gate(acc, o_ref, gate_ref):
    o_ref[...] = (acc * _silu(gate_ref[...])).astype(o_ref.dtype)


def _ep_resid(acc, o_ref, r_ref):
    o_ref[...] = (r_ref[...] + acc).astype(o_ref.dtype)


def _even_mid_kernel(au_ref, av_ref, ag_ref, ba_ref, bb_ref, alg_ref, alb_ref, ws_ref, wbt_ref, dw_ref, dwb_ref,
                     blg_ref, blb_ref, aout_ref, c_ref, tail_ref, wsb_ref, xp_ref, conv_ref):
    b = pl.program_id(0)
    c = pl.program_id(1)
    n_chunks = pl.num_programs(1)

    @pl.when(jnp.logical_and(b == 0, c == 0))
    def _():
        t_idx = lax.broadcasted_iota(jnp.int32, (CHUNK, CHUNK), 0)
        s_idx = lax.broadcasted_iota(jnp.int32, (CHUNK, CHUNK), 1)
        for g in range(A_GROUPS):
            wsb_ref[g] = jnp.where(s_idx <= t_idx, ws_ref[g], 0.0).astype(BF16)

    v_n = _layernorm(_gelu(av_ref[...]), alg_ref[...], alb_ref[...])
    v_b = v_n.astype(BF16)
    for g in range(A_GROUPS):
        sl = slice(g * A_GROUP_CH, (g + 1) * A_GROUP_CH)
        sp = _dot(wsb_ref[g], v_b[:, sl]) + wbt_ref[:, g:g + 1]
        aout_ref[:, sl] = (_gelu(au_ref[:, sl]) * sp * _silu(ag_ref[:, sl])).astype(aout_ref.dtype)

    @pl.when(c == 0)
    def _():
        xp_ref[0:CONV_PAD, :] = jnp.zeros((CONV_PAD, W_B), F32)

    xp_ref[CONV_PAD:CONV_PAD + CHUNK, :] = ba_ref[...] * _sigmoid(bb_ref[...])
    first = CONV_PAD - (CONV_W - 1)
    for cb in range(W_B // LANES):
        sl = slice(cb * LANES, (cb + 1) * LANES)
        conv = jnp.broadcast_to(dwb_ref[:, sl], (CHUNK, LANES))
        for r in range(SUBLANES):
            n_rows = CHUNK + (SUBLANES if r else 0)
            part = None
            for j in range(CONV_W):
                if (first + j) % SUBLANES == r:
                    a0 = first + j - r
                    term = xp_ref[a0:a0 + n_rows, sl] * dw_ref[j:j + 1, sl]
                    part = term if part is None else part + term
            conv = conv + part[r:r + CHUNK]
        conv_ref[:, sl] = conv
    c_ref[...] = _silu(_layernorm(conv_ref[...], blg_ref[...], blb_ref[...])).astype(c_ref.dtype)

    @pl.when(c == n_chunks - 1)
    def _():
        tail_ref[...] = xp_ref[CONV_PAD + CHUNK - (CONV_W - 1):CONV_PAD + CHUNK, :]

    xp_ref[0:CONV_PAD, :] = xp_ref[CHUNK:CHUNK + CONV_PAD, :]


def _even_mid_prompt(z, a_ln_g, a_ln_b, a_ws, a_wbt, b_dw, b_dw_bias, b_ln_g, b_ln_b):
    n_chunks = SEQ // CHUNK
    zspec = lambda k: pl.BlockSpec((CHUNK, W_A), functools.partial(lambda b, c, k: (b * n_chunks + c, k), k=k))
    row = lambda w: pl.BlockSpec((1, w), lambda b, c: (0, 0))
    return pl.pallas_call(
        _even_mid_kernel,
        grid=(BATCH, n_chunks),
        in_specs=[zspec(0), zspec(1), zspec(2), zspec(3), zspec(4), row(W_A), row(W_A),
                  pl.BlockSpec((A_GROUPS, CHUNK, CHUNK), lambda b, c: (0, 0, 0)),
                  pl.BlockSpec((CHUNK, A_GROUPS), lambda b, c: (0, 0)),
                  pl.BlockSpec((CONV_W, W_B), lambda b, c: (0, 0)), row(W_B), row(W_B), row(W_B)],
        out_specs=[pl.BlockSpec((CHUNK, W_A), lambda b, c: (b * n_chunks + c, 0)),
                   pl.BlockSpec((CHUNK, W_B), lambda b, c: (b * n_chunks + c, 0)),
                   pl.BlockSpec((None, CONV_W - 1, W_B), lambda b, c: (b, 0, 0))],
        out_shape=[jax.ShapeDtypeStruct((M_PROMPT, W_A), BF16), jax.ShapeDtypeStruct((M_PROMPT, W_B), BF16),
                   jax.ShapeDtypeStruct((BATCH, CONV_W - 1, W_B), F32)],
        scratch_shapes=[pltpu.VMEM((A_GROUPS, CHUNK, CHUNK), BF16), pltpu.VMEM((CONV_PAD + CHUNK, W_B), F32),
                        pltpu.VMEM((CHUNK, W_B), F32)],
        compiler_params=pltpu.CompilerParams(dimension_semantics=("arbitrary", "arbitrary"),
                                             vmem_limit_bytes=VMEM_LIMIT),
        name="even_mid",
    )(z, z, z, z, z, a_ln_g.reshape(1, W_A), a_ln_b.reshape(1, W_A), a_ws, a_wbt, b_dw, b_dw_bias.reshape(1, W_B),
      b_ln_g.reshape(1, W_B), b_ln_b.reshape(1, W_B))


def _even_mid_sample_kernel(z_ref, st_ref, alg_ref, alb_ref, ws0_ref, wb0_ref, dw_ref, dwb_ref, blg_ref, blb_ref,
                            aout_ref, c_ref, vn_ref, nst_ref, conv_ref):
    a_u = z_ref[:, 0:W_A]
    a_v = z_ref[:, W_A:2 * W_A]
    a_g = z_ref[:, 2 * W_A:3 * W_A]
    b_a = z_ref[:, 3 * W_A:3 * W_A + W_B]
    b_b = z_ref[:, 3 * W_A + W_B:3 * W_A + 2 * W_B]
    v_n = _layernorm(_gelu(a_v), alg_ref[...], alb_ref[...])
    vn_ref[...] = v_n
    sp = ws0_ref[...] * v_n + wb0_ref[...]
    aout_ref[...] = (_gelu(a_u) * sp * _silu(a_g)).astype(aout_ref.dtype)
    glu = b_a * _sigmoid(b_b)
    hist = CONV_W - 1
    for b in range(DEC_BATCH):
        st = st_ref[b]
        row = glu[b:b + 1, :]
        conv_ref[b:b + 1, :] = (jnp.sum(st * dw_ref[0:hist, :], axis=0, keepdims=True) + row * dw_ref[hist:CONV_W, :]
                                + dwb_ref[...])
        nst_ref[b, 0:hist - 1, :] = st[1:hist, :]
        nst_ref[b, hist - 1:hist, :] = row
    c_ref[...] = _silu(_layernorm(conv_ref[...], blg_ref[...], blb_ref[...])).astype(c_ref.dtype)


def _even_mid_sample(z, state, a_ln_g, a_ln_b, ws0, wb0, b_dw, b_dw_bias, b_ln_g, b_ln_b):
    return pl.pallas_call(
        _even_mid_sample_kernel,
        out_shape=[jax.ShapeDtypeStruct((DEC_BATCH, W_A), BF16), jax.ShapeDtypeStruct((DEC_BATCH, W_B), BF16),
                   jax.ShapeDtypeStruct((DEC_BATCH, W_A), F32),
                   jax.ShapeDtypeStruct((DEC_BATCH, CONV_W - 1, W_B), F32)],
        scratch_shapes=[pltpu.VMEM((DEC_BATCH, W_B), F32)],
        compiler_params=pltpu.CompilerParams(vmem_limit_bytes=VMEM_LIMIT),
        name="even_mid_sample",
    )(z, state, a_ln_g.reshape(1, W_A), a_ln_b.reshape(1, W_A), ws0, wb0, b_dw, b_dw_bias.reshape(1, W_B),
      b_ln_g.reshape(1, W_B), b_ln_b.reshape(1, W_B))


SCALE = HEAD_DIM ** -0.5
LOG2E = math.log2(math.e)


def _suffix_sum_matrix():
    j = jnp.arange(2 * LANES)[:, None] % LANES
    s = jnp.arange(2 * LANES)[None, :]
    return jnp.where(s < LANES, -(j >= s).astype(F32), -1.0).astype(BF16)


def _stick_sums(z, tmat, masks):
    e = jnp.exp2(jnp.abs(z) * (-LOG2E))
    sp = jnp.maximum(z, 0.0) + jnp.log(1.0 + e)
    cums = []
    for s in range(z.shape[1] // LANES):
        sp_s = sp[:, s * LANES:(s + 1) * LANES]
        if masks[s] is not None:
            sp_s = jnp.where(masks[s], sp_s, 0.0)
        hi = sp_s.astype(BF16)
        lo = (sp_s - hi.astype(F32)).astype(BF16)
        cums.append(_dot(jnp.concatenate([hi, lo], axis=1), tmat))
    return cums


def _stick_weights(z, cums, carry, masks):
    n_sub = len(cums)
    ws = [None] * n_sub
    for s in reversed(range(n_sub)):
        log_w = z[:, s * LANES:(s + 1) * LANES] + cums[s][:, :LANES]
        w_s = jnp.exp(log_w if carry is None else log_w + carry)
        if masks[s] is not None:
            w_s = jnp.where(masks[s], w_s, 0.0)
        ws[s] = w_s.astype(BF16)
        carry = cums[s][:, LANES:] if carry is None else carry + cums[s][:, LANES:]
    return ws, carry


def _weighted_values(acc, ws, v):
    out = _dot(ws[0] if len(ws) == 1 else jnp.concatenate(ws, axis=1), v)
    return out if acc is None else acc + out


def _attn_prompt_kernel(bias_ref, q_ref, k_ref, v_ref, g_ref, tmat_ref, o_ref, *, tq):
    bias = bias_ref[pl.program_id(1)]
    half = tq // 2
    n_sub = tq // LANES
    tmat = tmat_ref[...]
    row = lax.broadcasted_iota(jnp.int32, (half, LANES), 0)
    col = lax.broadcasted_iota(jnp.int32, (half, LANES), 1)
    near = col < row
    far = col + LANES < row

    work = []
    for q0 in range(0, SEQ, tq):
        work.append((q0, half, q0, half, [near, far]))
        work.append((q0 + half, half, q0, tq, [None, None, near, far]))
    for dist in range(tq, SEQ, tq):
        for q0 in range(dist, SEQ, tq):
            work.append((q0, tq, q0 - dist, tq, [None] * n_sub))

    k_bf, v_bf = {}, {}

    def keys_of(ref, cache, k0, n_keys):
        b0 = k0 - k0 % tq
        if b0 not in cache:
            cache[b0] = ref[b0:b0 + tq, :].astype(BF16)
        return cache[b0][k0 - b0:k0 - b0 + n_keys, :]

    state = {}
    zs, cums = {}, {}

    def stage_a(i):
        r0, n_rows, k0, n_keys, _ = work[i]
        zs[i] = _dot_nt(q_ref[r0:r0 + n_rows, :], keys_of(k_ref, k_bf, k0, n_keys)) + bias

    def stage_b(i):
        cums[i] = _stick_sums(zs[i], tmat, work[i][4])

    def stage_c(i):
        r0, n_rows, k0, n_keys, masks = work[i]
        halves = list(range(r0, r0 + n_rows, half))
        if halves[0] in state:
            carry = jnp.concatenate([state[r][0] for r in halves], axis=0) if len(halves) > 1 else state[r0][0]
            acc = jnp.concatenate([state[r][1] for r in halves], axis=0) if len(halves) > 1 else state[r0][1]
        else:
            carry, acc = None, None
        ws, carry = _stick_weights(zs.pop(i), cums.pop(i), carry, masks)
        acc = _weighted_values(acc, ws, keys_of(v_ref, v_bf, k0, n_keys))
        for n, r in enumerate(halves):
            state[r] = (carry[n * half:(n + 1) * half], acc[n * half:(n + 1) * half])

    for t in range(len(work) + 2):
        if t < len(work):
            stage_a(t)
        if 0 <= t - 1 < len(work):
            stage_b(t - 1)
        if 0 <= t - 2 < len(work):
            stage_c(t - 2)

    for r in range(0, SEQ, half):
        o_ref[r:r + half, :] = (state[r][1] * _silu(g_ref[r:r + half, :])).astype(o_ref.dtype)


def _attn_prompt(q, k, v, gate, bias, tmat, tq):
    assert tq == 4 * LANES and SEQ % tq == 0
    per_head = pl.BlockSpec((SEQ, HEAD_DIM), lambda b, h: (b, h))
    return pl.pallas_call(
        functools.partial(_attn_prompt_kernel, tq=tq),
        grid=(BATCH, N_HEADS_C),
        in_specs=[pl.BlockSpec(memory_space=pltpu.SMEM), per_head, per_head, per_head, per_head,
                  pl.BlockSpec((2 * LANES, 2 * LANES), lambda b, h: (0, 0))],
        out_specs=per_head,
        out_shape=jax.ShapeDtypeStruct((M_PROMPT, W_C), BF16),
        compiler_params=pltpu.CompilerParams(dimension_semantics=("parallel", "parallel"),
                                             vmem_limit_bytes=VMEM_LIMIT),
        name="attn_prompt",
    )(bias, q, k, v, gate, tmat)


PAGE_ROWS = PAGE_SIZE * N_HEADS_C


def _page_select_matrices():
    pos_of_row = jnp.arange(PAGE_ROWS)[:, None] // N_HEADS_C
    gather = (pos_of_row == jnp.arange(PAGE_SIZE)[None, :]).astype(BF16)
    return gather, gather.T


PAGES_PER_STEP = 4


def _attn_sample_kernel(pt_ref, q_ref, kn_ref, vn_ref, g_ref, bias_ref, tmat_ref, gat_ref, sca_ref, *refs):
    kc_refs = refs[:PAGES_PER_STEP]
    vc_refs = refs[PAGES_PER_STEP:2 * PAGES_PER_STEP]
    o_ref, acc_ref, carry_ref = refs[2 * PAGES_PER_STEP:]
    p = pl.program_id(1)
    bias = bias_ref[...]
    own = (lax.broadcasted_iota(jnp.int32, (N_HEADS_C, PAGE_ROWS), 1) % N_HEADS_C
           == lax.broadcasted_iota(jnp.int32, (N_HEADS_C, PAGE_ROWS), 0))

    @pl.when(p == 0)
    def _():
        lane = lax.broadcasted_iota(jnp.int32, (N_HEADS_C, PAGE_SIZE), 1)
        mask = jnp.logical_and(lane < DEC_SEQ, PAST_LEN + lane < PAST_LEN + DEC_SEQ - 1)
        z = jnp.sum(q_ref[...] * kn_ref[...], axis=1, keepdims=True) + bias
        ws, carry = _stick_weights(z, _stick_sums(z, tmat_ref[...], [mask]), jnp.zeros((N_HEADS_C, LANES), F32),
                                   [mask])
        carry_ref[...] = carry
        acc_ref[...] = jnp.sum(ws[0].astype(F32), axis=1, keepdims=True) * vn_ref[...]

    @pl.when(p > 0)
    def _():
        qb = q_ref[...].astype(BF16)
        zs, cums = [], []
        for kc_ref in kc_refs:
            k_rows = kc_ref[...].reshape(PAGE_ROWS, HEAD_DIM).astype(BF16)
            z_all = jnp.where(own, _dot_nt(qb, k_rows), 0.0)
            hi = z_all.astype(BF16)
            lo = (z_all - hi.astype(F32)).astype(BF16)
            zc = _dot(jnp.concatenate([hi, lo], axis=0), gat_ref[...])
            zs.append(zc[:N_HEADS_C] + zc[N_HEADS_C:] + bias)
        for z in zs:
            cums.append(_stick_sums(z, tmat_ref[...], [None]))
        carry = carry_ref[...]
        acc = acc_ref[...]
        for z, cum, vc_ref in zip(zs, cums, vc_refs):
            ws, carry = _stick_weights(z, cum, carry, [None])
            w_all = jnp.where(own, _dot(ws[0], sca_ref[...]), 0.0).astype(BF16)
            acc = acc + _dot(w_all, vc_ref[...].reshape(PAGE_ROWS, HEAD_DIM).astype(BF16))
        carry_ref[...] = carry
        acc_ref[...] = acc

    @pl.when(p == pl.num_programs(1) - 1)
    def _():
        o_ref[...] = (acc_ref[...] * _silu(g_ref[...])).astype(o_ref.dtype)


def _attn_sample(page_table, q, k_new, v_new, gate, bias_rows, tmat, cache_k, cache_v, layer):
    gather, scatter = _page_select_matrices()
    per_b = pl.BlockSpec((None, N_HEADS_C, HEAD_DIM), lambda b, p, pt: (b, 0, 0))
    const = lambda shape: pl.BlockSpec(shape, lambda b, p, pt: (0, 0))
    assert N_PAGES % PAGES_PER_STEP == 0

    def cache(r):
        return pl.BlockSpec(
            (None, None, PAGE_SIZE, N_HEADS_C, HEAD_DIM),
            lambda b, p, pt: (layer, pt[b, N_PAGES - 1 - r - PAGES_PER_STEP * (jnp.maximum(p, 1) - 1)], 0, 0, 0))

    caches = [cache(r) for r in range(PAGES_PER_STEP)]
    grid_spec = pltpu.PrefetchScalarGridSpec(
        num_scalar_prefetch=1,
        grid=(DEC_BATCH, N_PAGES // PAGES_PER_STEP + 1),
        in_specs=[per_b, per_b, per_b, per_b, const((N_HEADS_C, LANES)), const((2 * LANES, 2 * LANES)),
                  const((PAGE_ROWS, PAGE_SIZE)), const((PAGE_SIZE, PAGE_ROWS))] + caches + caches,
        out_specs=per_b,
        scratch_shapes=[pltpu.VMEM((N_HEADS_C, HEAD_DIM), F32), pltpu.VMEM((N_HEADS_C, LANES), F32)],
    )
    return pl.pallas_call(
        _attn_sample_kernel,
        grid_spec=grid_spec,
        out_shape=jax.ShapeDtypeStruct((DEC_BATCH, N_HEADS_C, HEAD_DIM), BF16),
        compiler_params=pltpu.CompilerParams(dimension_semantics=("parallel", "arbitrary"),
                                             vmem_limit_bytes=VMEM_LIMIT),
        name="attn_sample",
    )(page_table, q, k_new, v_new, gate, bias_rows, tmat, gather, scatter, *([cache_k] * PAGES_PER_STEP),
      *([cache_v] * PAGES_PER_STEP))


TM = 1024
TN = 512
TQ = 512
RMS_ROWS = 256


def _even_layer(xp, xs, state, i, norm_g, w_in, a_ln_g, a_ln_b, a_ws, a_wb, b_dw, b_dw_bias, b_ln_g, b_ln_b, b_pw,
                w_out):
    a_wbt = a_wb[i].T
    ws0 = jnp.repeat(a_ws[i][:, 0, 0], A_GROUP_CH).reshape(1, W_A)
    wb0 = jnp.repeat(a_wb[i][:, 0], A_GROUP_CH).reshape(1, W_A)
    hp = _rmsnorm(xp, norm_g[i], RMS_ROWS)
    hs = _rmsnorm(xs, norm_g[i], DEC_BATCH)
    zp, zs = _matmul([hp], [hs], w_in, i, [0], 0, EVEN_IN, tm=TM, tn=TN, epilogue=_ep_plain, name="even_in")
    a_p, c_p, tail = _even_mid_prompt(zp, a_ln_g[i], a_ln_b[i], a_ws[i], a_wbt, b_dw[i], b_dw_bias[i], b_ln_g[i],
                                      b_ln_b[i])
    a_s, c_s, v_n, new_state = _even_mid_sample(zs, state[i], a_ln_g[i], a_ln_b[i], ws0, wb0, b_dw[i], b_dw_bias[i],
                                                b_ln_g[i], b_ln_b[i])
    b_p, b_s = _matmul([c_p], [c_s], b_pw, i, [0], 0, W_B, tm=TM, tn=TN, epilogue=_ep_gate,
                       row_extras=[(zp, zs, 3 * W_A + 2 * W_B)], out_dtype=BF16, name="even_pw")
    xp_new, xs_new = _matmul([a_p, b_p], [a_s, b_s], w_out, i, [0, 1], 0, D_MODEL, tm=TM, tn=TN, epilogue=_ep_resid,
                             row_extras=[(xp, xs, 0)], name="even_out")
    return xp_new, xs_new, v_n, tail, new_state


def _odd_layer(xp, xs, cache_k, cache_v, page_table, i, norm_g, w_in, q_norm, k_norm, c_bias, w_out, tmat):
    qg = q_norm[i].reshape(1, HEAD_DIM)
    kg = k_norm[i].reshape(1, HEAD_DIM)
    hp = _rmsnorm(xp, norm_g[i], RMS_ROWS)
    hs = _rmsnorm(xs, norm_g[i], DEC_BATCH)
    mm = functools.partial(_matmul, [hp], [hs], w_in, i, [0], tm=TM, tn=TN)
    q_p, q_s = mm(0, W_C, epilogue=functools.partial(_ep_headnorm, mult=SCALE), params=[qg], out_dtype=BF16,
                  name="odd_q")
    k_p, k_s = mm(W_C, W_C, epilogue=functools.partial(_ep_headnorm, mult=1.0), params=[kg], name="odd_k")
    v_p, v_s = mm(2 * W_C, W_C, epilogue=_ep_plain, name="odd_v")
    g_p, g_s = mm(3 * W_C, W_C, epilogue=_ep_plain, name="odd_g")
    o_p = _attn_prompt(q_p, k_p, v_p, g_p, c_bias[i].astype(F32), tmat, TQ)
    r3 = lambda a: a.reshape(DEC_BATCH, N_HEADS_C, HEAD_DIM)
    bias_rows = jnp.broadcast_to(c_bias[i].astype(F32)[:, None], (N_HEADS_C, LANES))
    o_s = _attn_sample(page_table, r3(q_s), r3(k_s), r3(v_s), r3(g_s), bias_rows, tmat, cache_k, cache_v, i)
    xp_new, xs_new = _matmul([o_p], [o_s.reshape(DEC_BATCH, W_C)], w_out, i, [0], 0, D_MODEL, tm=TM, tn=TN,
                             epilogue=_ep_resid, row_extras=[(xp, xs, 0)], name="odd_out")
    return xp_new, k_p, v_p, xs_new, k_s, v_s


def kernel(x_prompt, x_sample, state_conv, cache_k, cache_v, page_table, norm_even, w_in_even, a_ln_g, a_ln_b, a_ws,
           a_wb, b_dw, b_dw_bias, b_ln_g, b_ln_b, b_pw, w_out_even, norm_odd, w_in_odd, q_norm, k_norm, c_bias,
           w_out_odd):
    xp = x_prompt.reshape(M_PROMPT, D_MODEL)
    xs = x_sample.reshape(DEC_BATCH * DEC_SEQ, D_MODEL)
    tmat = _suffix_sum_matrix()
    a_v_s, conv_p, conv_s, k_p, v_p, k_s, v_s = [], [], [], [], [], [], []
    for layer in range(DEPTH):
        i = layer // 2
        if layer % 2 == 0:
            xp, xs, v_n, tail, new_state = _even_layer(xp, xs, state_conv, i, norm_even, w_in_even, a_ln_g, a_ln_b,
                                                       a_ws, a_wb, b_dw, b_dw_bias, b_ln_g, b_ln_b, b_pw, w_out_even)
            a_v_s.append(v_n.reshape(DEC_BATCH, DEC_SEQ, W_A))
            conv_p.append(tail)
            conv_s.append(new_state)
        else:
            xp, kp, vp, xs, kn, vn = _odd_layer(xp, xs, cache_k, cache_v, page_table, i, norm_odd, w_in_odd, q_norm, k_norm,
                                                c_bias, w_out_odd, tmat)
            k_p.append(kp.reshape(BATCH, SEQ, N_HEADS_C, HEAD_DIM))
            v_p.append(vp.reshape(BATCH, SEQ, N_HEADS_C, HEAD_DIM))
            k_s.append(kn.reshape(DEC_BATCH, DEC_SEQ, N_HEADS_C, HEAD_DIM))
            v_s.append(vn.reshape(DEC_BATCH, DEC_SEQ, N_HEADS_C, HEAD_DIM))
    return (xp.reshape(BATCH, SEQ, D_MODEL), xs.reshape(DEC_BATCH, DEC_SEQ, D_MODEL), jnp.stack(a_v_s),
            jnp.stack(conv_p), jnp.stack(conv_s), jnp.stack(k_p), jnp.stack(v_p), jnp.stack(k_s), jnp.stack(v_s))
```

```python
import functools
import math

import jax
import jax.numpy as jnp
from jax import lax
from jax.experimental import pallas as pl
from jax.experimental.pallas import tpu as pltpu

D_MODEL = 4096
BATCH = 4
SEQ = 2048
DEPTH = 4
DEC_BATCH = 8
DEC_SEQ = 1
PAST_LEN = 8192
PAGE_SIZE = 128
N_EVEN = (DEPTH + 1) // 2
N_ODD = DEPTH // 2
W_A = D_MODEL // 2
W_B = D_MODEL // 2
CHUNK = 128
A_GROUP_CH = 128
A_GROUPS = W_A // A_GROUP_CH
CONV_W = 31
W_C = D_MODEL
HEAD_DIM = 128
N_HEADS_C = W_C // HEAD_DIM
EPS = 1e-6
EVEN_IN = 3 * W_A + 3 * W_B
ODD_IN = 4 * W_C
N_PAGES = PAST_LEN // PAGE_SIZE

LANES = 128
SUBLANES = 8
VMEM_LIMIT = 56 * 1024 * 1024

F32 = jnp.float32
BF16 = jnp.bfloat16

M_PROMPT = BATCH * SEQ
CONV_PAD = 32

assert DEC_SEQ == 1 and CHUNK == LANES and HEAD_DIM == LANES and A_GROUP_CH == LANES


def _gelu(x):
    return 0.5 * x * (1.0 + jnp.tanh(math.sqrt(2.0 / math.pi) * (x + 0.044715 * (x * x * x))))


def _sigmoid(x):
    return 1.0 / (1.0 + jnp.exp(-x))


def _silu(x):
    return x * _sigmoid(x)


def _layernorm(x, g, b):
    xc = x - jnp.mean(x, axis=-1, keepdims=True)
    y = xc * lax.rsqrt(jnp.mean(xc * xc, axis=-1, keepdims=True) + EPS)
    return y * g + b


def _dot(a, b):
    return jnp.dot(a, b, preferred_element_type=F32)


def _dot_nt(a, b):
    return lax.dot_general(a, b, (((1,), (1,)), ((), ())), preferred_element_type=F32)


def _rmsnorm_kernel(x_ref, g_ref, o_ref):
    x = x_ref[...]
    y = x * lax.rsqrt(jnp.mean(x * x, axis=-1, keepdims=True) + EPS)
    o_ref[...] = (y * g_ref[...]).astype(o_ref.dtype)


def _rmsnorm(x, g, tm):
    m, d = x.shape
    return pl.pallas_call(
        _rmsnorm_kernel,
        grid=(m // tm,),
        in_specs=[pl.BlockSpec((tm, d), lambda i: (i, 0)), pl.BlockSpec((1, d), lambda i: (0, 0))],
        out_specs=pl.BlockSpec((tm, d), lambda i: (i, 0)),
        out_shape=jax.ShapeDtypeStruct((m, d), BF16),
        compiler_params=pltpu.CompilerParams(dimension_semantics=("parallel",), vmem_limit_bytes=VMEM_LIMIT),
        name="rmsnorm",
    )(x, g.reshape(1, d))


def _mm_kernel(*refs, n_x, n_extra, epilogue):
    x_refs, xs_refs, w_refs = refs[:n_x], refs[n_x:2 * n_x], refs[2 * n_x:3 * n_x]
    extra_refs = refs[3 * n_x:3 * n_x + n_extra]
    extra_s_refs = refs[3 * n_x + n_extra:3 * n_x + 2 * n_extra]
    o_ref, os_ref = refs[3 * n_x + 2 * n_extra:3 * n_x + 2 * n_extra + 2]
    wbf_refs = refs[3 * n_x + 2 * n_extra + 2:]

    def product(rows):
        acc = _dot(rows[0][...], wbf_refs[0][...])
        for x_ref, wbf_ref in zip(rows[1:], wbf_refs[1:]):
            acc = acc + _dot(x_ref[...], wbf_ref[...])
        return acc

    @pl.when(pl.program_id(1) == 0)
    def _():
        for w_ref, wbf_ref in zip(w_refs, wbf_refs):
            wbf_ref[...] = w_ref[...].astype(BF16)
        epilogue(product(xs_refs), os_ref, *extra_s_refs)

    epilogue(product(x_refs), o_ref, *extra_refs)


def _matmul(xs, xs_s, w, layer, row_blocks, col_off, n, *, tm, tn, epilogue, row_extras=(), params=(),
            out_dtype=F32, name="matmul"):
    m, ms = xs[0].shape[0], xs_s[0].shape[0]
    cb = col_off // tn
    in_specs = [pl.BlockSpec((tm, x.shape[1]), lambda j, i: (i, 0)) for x in xs]
    in_specs += [pl.BlockSpec((ms, x.shape[1]), lambda j, i: (0, 0)) for x in xs_s]
    for x, rb in zip(xs, row_blocks):
        in_specs.append(pl.BlockSpec((None, x.shape[1], tn), functools.partial(lambda j, i, rb: (layer, rb, j + cb), rb=rb)))
    param_specs = [pl.BlockSpec(p.shape, lambda j, i: (0, 0)) for p in params]
    in_specs += [pl.BlockSpec((tm, tn), functools.partial(lambda j, i, c: (i, j + c), c=c0 // tn))
                 for _, _, c0 in row_extras] + param_specs
    in_specs += [pl.BlockSpec((ms, tn), functools.partial(lambda j, i, c: (0, j + c), c=c0 // tn))
                 for _, _, c0 in row_extras] + param_specs
    kern = functools.partial(_mm_kernel, n_x=len(xs), n_extra=len(row_extras) + len(params), epilogue=epilogue)
    return pl.pallas_call(
        kern,
        grid=(n // tn, m // tm),
        in_specs=in_specs,
        out_specs=[pl.BlockSpec((tm, tn), lambda j, i: (i, j)), pl.BlockSpec((ms, tn), lambda j, i: (0, j))],
        out_shape=[jax.ShapeDtypeStruct((m, n), out_dtype), jax.ShapeDtypeStruct((ms, n), out_dtype)],
        scratch_shapes=[pltpu.VMEM((x.shape[1], tn), BF16) for x in xs],
        compiler_params=pltpu.CompilerParams(dimension_semantics=("parallel", "arbitrary"),
                                             vmem_limit_bytes=VMEM_LIMIT),
        name=name,
    )(*xs, *xs_s, *([w] * len(xs)), *[a for a, _, _ in row_extras], *params, *[a for _, a, _ in row_extras],
      *params)


def _ep_plain(acc, o_ref):
    o_ref[...] = acc.astype(o_ref.dtype)


def _ep_headnorm(acc, o_ref, g_ref, *, mult):
    g = g_ref[...] * mult
    for h in range(acc.shape[1] // HEAD_DIM):
        a = acc[:, h * HEAD_DIM:(h + 1) * HEAD_DIM]
        y = a * lax.rsqrt(jnp.mean(a * a, axis=-1, keepdims=True) + EPS)
        o_ref[:, h * HEAD_DIM:(h + 1) * HEAD_DIM] = (y * g).astype(o_ref.dtype)


def _ep_gate(acc, o_ref, gate_ref):
    o_ref[...] = (acc * _silu(gate_ref[...])).astype(o_ref.dtype)


def _ep_resid(acc, o_ref, r_ref):
    o_ref[...] = (r_ref[...] + acc).astype(o_ref.dtype)


def _even_mid_kernel(au_ref, av_ref, ag_ref, ba_ref, bb_ref, alg_ref, alb_ref, ws_ref, wbt_ref, dw_ref, dwb_ref,
                     blg_ref, blb_ref, aout_ref, c_ref, tail_ref, wsb_ref, xp_ref, conv_ref):
    b = pl.program_id(0)
    c = pl.program_id(1)
    n_chunks = pl.num_programs(1)

    @pl.when(jnp.logical_and(b == 0, c == 0))
    def _():
        t_idx = lax.broadcasted_iota(jnp.int32, (CHUNK, CHUNK), 0)
        s_idx = lax.broadcasted_iota(jnp.int32, (CHUNK, CHUNK), 1)
        for g in range(A_GROUPS):
            wsb_ref[g] = jnp.where(s_idx <= t_idx, ws_ref[g], 0.0).astype(BF16)

    v_n = _layernorm(_gelu(av_ref[...]), alg_ref[...], alb_ref[...])
    v_b = v_n.astype(BF16)
    for g in range(A_GROUPS):
        sl = slice(g * A_GROUP_CH, (g + 1) * A_GROUP_CH)
        sp = _dot(wsb_ref[g], v_b[:, sl]) + wbt_ref[:, g:g + 1]
        aout_ref[:, sl] = (_gelu(au_ref[:, sl]) * sp * _silu(ag_ref[:, sl])).astype(aout_ref.dtype)

    @pl.when(c == 0)
    def _():
        xp_ref[0:CONV_PAD, :] = jnp.zeros((CONV_PAD, W_B), F32)

    xp_ref[CONV_PAD:CONV_PAD + CHUNK, :] = ba_ref[...] * _sigmoid(bb_ref[...])
    first = CONV_PAD - (CONV_W - 1)
    for cb in range(W_B // LANES):
        sl = slice(cb * LANES, (cb + 1) * LANES)
        conv = jnp.broadcast_to(dwb_ref[:, sl], (CHUNK, LANES))
        for r in range(SUBLANES):
            n_rows = CHUNK + (SUBLANES if r else 0)
            part = None
            for j in range(CONV_W):
                if (first + j) % SUBLANES == r:
                    a0 = first + j - r
                    term = xp_ref[a0:a0 + n_rows, sl] * dw_ref[j:j + 1, sl]
                    part = term if part is None else part + term
            conv = conv + part[r:r + CHUNK]
        conv_ref[:, sl] = conv
    c_ref[...] = _silu(_layernorm(conv_ref[...], blg_ref[...], blb_ref[...])).astype(c_ref.dtype)

    @pl.when(c == n_chunks - 1)
    def _():
        tail_ref[...] = xp_ref[CONV_PAD + CHUNK - (CONV_W - 1):CONV_PAD + CHUNK, :]

    xp_ref[0:CONV_PAD, :] = xp_ref[CHUNK:CHUNK + CONV_PAD, :]


def _even_mid_prompt(z, a_ln_g, a_ln_b, a_ws, a_wbt, b_dw, b_dw_bias, b_ln_g, b_ln_b):
    n_chunks = SEQ // CHUNK
    zspec = lambda k: pl.BlockSpec((CHUNK, W_A), functools.partial(lambda b, c, k: (b * n_chunks + c, k), k=k))
    row = lambda w: pl.BlockSpec((1, w), lambda b, c: (0, 0))
    return pl.pallas_call(
        _even_mid_kernel,
        grid=(BATCH, n_chunks),
        in_specs=[zspec(0), zspec(1), zspec(2), zspec(3), zspec(4), row(W_A), row(W_A),
                  pl.BlockSpec((A_GROUPS, CHUNK, CHUNK), lambda b, c: (0, 0, 0)),
                  pl.BlockSpec((CHUNK, A_GROUPS), lambda b, c: (0, 0)),
                  pl.BlockSpec((CONV_W, W_B), lambda b, c: (0, 0)), row(W_B), row(W_B), row(W_B)],
        out_specs=[pl.BlockSpec((CHUNK, W_A), lambda b, c: (b * n_chunks + c, 0)),
                   pl.BlockSpec((CHUNK, W_B), lambda b, c: (b * n_chunks + c, 0)),
                   pl.BlockSpec((None, CONV_W - 1, W_B), lambda b, c: (b, 0, 0))],
        out_shape=[jax.ShapeDtypeStruct((M_PROMPT, W_A), BF16), jax.ShapeDtypeStruct((M_PROMPT, W_B), BF16),
                   jax.ShapeDtypeStruct((BATCH, CONV_W - 1, W_B), F32)],
        scratch_shapes=[pltpu.VMEM((A_GROUPS, CHUNK, CHUNK), BF16), pltpu.VMEM((CONV_PAD + CHUNK, W_B), F32),
                        pltpu.VMEM((CHUNK, W_B), F32)],
        compiler_params=pltpu.CompilerParams(dimension_semantics=("arbitrary", "arbitrary"),
                                             vmem_limit_bytes=VMEM_LIMIT),
        name="even_mid",
    )(z, z, z, z, z, a_ln_g.reshape(1, W_A), a_ln_b.reshape(1, W_A), a_ws, a_wbt, b_dw, b_dw_bias.reshape(1, W_B),
      b_ln_g.reshape(1, W_B), b_ln_b.reshape(1, W_B))


def _even_mid_sample_kernel(z_ref, st_ref, alg_ref, alb_ref, ws0_ref, wb0_ref, dw_ref, dwb_ref, blg_ref, blb_ref,
                            aout_ref, c_ref, vn_ref, nst_ref, conv_ref):
    a_u = z_ref[:, 0:W_A]
    a_v = z_ref[:, W_A:2 * W_A]
    a_g = z_ref[:, 2 * W_A:3 * W_A]
    b_a = z_ref[:, 3 * W_A:3 * W_A + W_B]
    b_b = z_ref[:, 3 * W_A + W_B:3 * W_A + 2 * W_B]
    v_n = _layernorm(_gelu(a_v), alg_ref[...], alb_ref[...])
    vn_ref[...] = v_n
    sp = ws0_ref[...] * v_n + wb0_ref[...]
    aout_ref[...] = (_gelu(a_u) * sp * _silu(a_g)).astype(aout_ref.dtype)
    glu = b_a * _sigmoid(b_b)
    hist = CONV_W - 1
    for b in range(DEC_BATCH):
        st = st_ref[b]
        row = glu[b:b + 1, :]
        conv_ref[b:b + 1, :] = (jnp.sum(st * dw_ref[0:hist, :], axis=0, keepdims=True) + row * dw_ref[hist:CONV_W, :]
                                + dwb_ref[...])
        nst_ref[b, 0:hist - 1, :] = st[1:hist, :]
        nst_ref[b, hist - 1:hist, :] = row
    c_ref[...] = _silu(_layernorm(conv_ref[...], blg_ref[...], blb_ref[...])).astype(c_ref.dtype)


def _even_mid_sample(z, state, a_ln_g, a_ln_b, ws0, wb0, b_dw, b_dw_bias, b_ln_g, b_ln_b):
    return pl.pallas_call(
        _even_mid_sample_kernel,
        out_shape=[jax.ShapeDtypeStruct((DEC_BATCH, W_A), BF16), jax.ShapeDtypeStruct((DEC_BATCH, W_B), BF16),
                   jax.ShapeDtypeStruct((DEC_BATCH, W_A), F32),
                   jax.ShapeDtypeStruct((DEC_BATCH, CONV_W - 1, W_B), F32)],
        scratch_shapes=[pltpu.VMEM((DEC_BATCH, W_B), F32)],
        compiler_params=pltpu.CompilerParams(vmem_limit_bytes=VMEM_LIMIT),
        name="even_mid_sample",
    )(z, state, a_ln_g.reshape(1, W_A), a_ln_b.reshape(1, W_A), ws0, wb0, b_dw, b_dw_bias.reshape(1, W_B),
      b_ln_g.reshape(1, W_B), b_ln_b.reshape(1, W_B))


SCALE = HEAD_DIM ** -0.5
LOG2E = math.log2(math.e)


def _suffix_sum_matrix():
    j = jnp.arange(2 * LANES)[:, None] % LANES
    s = jnp.arange(2 * LANES)[None, :]
    return jnp.where(s < LANES, -(j >= s).astype(F32), -1.0).astype(BF16)


def _stick_sums(z, tmat, masks):
    e = jnp.exp2(jnp.abs(z) * (-LOG2E))
    sp = jnp.maximum(z, 0.0) + jnp.log(1.0 + e)
    cums = []
    for s in range(z.shape[1] // LANES):
        sp_s = sp[:, s * LANES:(s + 1) * LANES]
        if masks[s] is not None:
            sp_s = jnp.where(masks[s], sp_s, 0.0)
        hi = sp_s.astype(BF16)
        lo = (sp_s - hi.astype(F32)).astype(BF16)
        cums.append(_dot(jnp.concatenate([hi, lo], axis=1), tmat))
    return cums


def _stick_weights(z, cums, carry, masks):
    n_sub = len(cums)
    ws = [None] * n_sub
    for s in reversed(range(n_sub)):
        log_w = z[:, s * LANES:(s + 1) * LANES] + cums[s][:, :LANES]
        w_s = jnp.exp(log_w if carry is None else log_w + carry)
        if masks[s] is not None:
            w_s = jnp.where(masks[s], w_s, 0.0)
        ws[s] = w_s.astype(BF16)
        carry = cums[s][:, LANES:] if carry is None else carry + cums[s][:, LANES:]
    return ws, carry


def _weighted_values(acc, ws, v):
    out = _dot(ws[0] if len(ws) == 1 else jnp.concatenate(ws, axis=1), v)
    return out if acc is None else acc + out


def _attn_prompt_kernel(bias_ref, q_ref, k_ref, v_ref, g_ref, tmat_ref, o_ref, *, tq):
    bias = bias_ref[pl.program_id(1)]
    half = tq // 2
    n_sub = tq // LANES
    tmat = tmat_ref[...]
    row = lax.broadcasted_iota(jnp.int32, (half, LANES), 0)
    col = lax.broadcasted_iota(jnp.int32, (half, LANES), 1)
    near = col < row
    far = col + LANES < row

    work = []
    for q0 in range(0, SEQ, tq):
        work.append((q0, half, q0, half, [near, far]))
        work.append((q0 + half, half, q0, tq, [None, None, near, far]))
    for dist in range(tq, SEQ, tq):
        for q0 in range(dist, SEQ, tq):
            work.append((q0, tq, q0 - dist, tq, [None] * n_sub))

    k_bf, v_bf = {}, {}

    def keys_of(ref, cache, k0, n_keys):
        b0 = k0 - k0 % tq
        if b0 not in cache:
            cache[b0] = ref[b0:b0 + tq, :].astype(BF16)
        return cache[b0][k0 - b0:k0 - b0 + n_keys, :]

    state = {}
    zs, cums = {}, {}

    def stage_a(i):
        r0, n_rows, k0, n_keys, _ = work[i]
        zs[i] = _dot_nt(q_ref[r0:r0 + n_rows, :], keys_of(k_ref, k_bf, k0, n_keys)) + bias

    def stage_b(i):
        cums[i] = _stick_sums(zs[i], tmat, work[i][4])

    def stage_c(i):
        r0, n_rows, k0, n_keys, masks = work[i]
        halves = list(range(r0, r0 + n_rows, half))
        if halves[0] in state:
            carry = jnp.concatenate([state[r][0] for r in halves], axis=0) if len(halves) > 1 else state[r0][0]
            acc = jnp.concatenate([state[r][1] for r in halves], axis=0) if len(halves) > 1 else state[r0][1]
        else:
            carry, acc = None, None
        ws, carry = _stick_weights(zs.pop(i), cums.pop(i), carry, masks)
        acc = _weighted_values(acc, ws, keys_of(v_ref, v_bf, k0, n_keys))
        for n, r in enumerate(halves):
            state[r] = (carry[n * half:(n + 1) * half], acc[n * half:(n + 1) * half])

    for t in range(len(work) + 2):
        if t < len(work):
            stage_a(t)
        if 0 <= t - 1 < len(work):
            stage_b(t - 1)
        if 0 <= t - 2 < len(work):
            stage_c(t - 2)

    for r in range(0, SEQ, half):
        o_ref[r:r + half, :] = (state[r][1] * _silu(g_ref[r:r + half, :])).astype(o_ref.dtype)


def _attn_prompt(q, k, v, gate, bias, tmat, tq):
    assert tq == 4 * LANES and SEQ % tq == 0
    per_head = pl.BlockSpec((SEQ, HEAD_DIM), lambda b, h: (b, h))
    return pl.pallas_call(
        functools.partial(_attn_prompt_kernel, tq=tq),
        grid=(BATCH, N_HEADS_C),
        in_specs=[pl.BlockSpec(memory_space=pltpu.SMEM), per_head, per_head, per_head, per_head,
                  pl.BlockSpec((2 * LANES, 2 * LANES), lambda b, h: (0, 0))],
        out_specs=per_head,
        out_shape=jax.ShapeDtypeStruct((M_PROMPT, W_C), BF16),
        compiler_params=pltpu.CompilerParams(dimension_semantics=("parallel", "parallel"),
                                             vmem_limit_bytes=VMEM_LIMIT),
        name="attn_prompt",
    )(bias, q, k, v, gate, tmat)


PAGE_ROWS = PAGE_SIZE * N_HEADS_C


def _page_select_matrices():
    pos_of_row = jnp.arange(PAGE_ROWS)[:, None] // N_HEADS_C
    gather = (pos_of_row == jnp.arange(PAGE_SIZE)[None, :]).astype(BF16)
    return gather, gather.T


PAGES_PER_STEP = 4


def _attn_sample_kernel(pt_ref, q_ref, kn_ref, vn_ref, g_ref, bias_ref, tmat_ref, gat_ref, sca_ref, *refs):
    kc_refs = refs[:PAGES_PER_STEP]
    vc_refs = refs[PAGES_PER_STEP:2 * PAGES_PER_STEP]
    o_ref, acc_ref, carry_ref = refs[2 * PAGES_PER_STEP:]
    p = pl.program_id(1)
    bias = bias_ref[...]
    own = (lax.broadcasted_iota(jnp.int32, (N_HEADS_C, PAGE_ROWS), 1) % N_HEADS_C
           == lax.broadcasted_iota(jnp.int32, (N_HEADS_C, PAGE_ROWS), 0))

    @pl.when(p == 0)
    def _():
        lane = lax.broadcasted_iota(jnp.int32, (N_HEADS_C, PAGE_SIZE), 1)
        mask = jnp.logical_and(lane < DEC_SEQ, PAST_LEN + lane < PAST_LEN + DEC_SEQ - 1)
        z = jnp.sum(q_ref[...] * kn_ref[...], axis=1, keepdims=True) + bias
        ws, carry = _stick_weights(z, _stick_sums(z, tmat_ref[...], [mask]), jnp.zeros((N_HEADS_C, LANES), F32),
                                   [mask])
        carry_ref[...] = carry
        acc_ref[...] = jnp.sum(ws[0].astype(F32), axis=1, keepdims=True) * vn_ref[...]

    @pl.when(p > 0)
    def _():
        qb = q_ref[...].astype(BF16)
        zs, cums = [], []
        for kc_ref in kc_refs:
            k_rows = kc_ref[...].reshape(PAGE_ROWS, HEAD_DIM).astype(BF16)
            z_all = jnp.where(own, _dot_nt(qb, k_rows), 0.0)
            hi = z_all.astype(BF16)
            lo = (z_all - hi.astype(F32)).astype(BF16)
            zc = _dot(jnp.concatenate([hi, lo], axis=0), gat_ref[...])
            zs.append(zc[:N_HEADS_C] + zc[N_HEADS_C:] + bias)
        for z in zs:
            cums.append(_stick_sums(z, tmat_ref[...], [None]))
        carry = carry_ref[...]
        acc = acc_ref[...]
        for z, cum, vc_ref in zip(zs, cums, vc_refs):
            ws, carry = _stick_weights(z, cum, carry, [None])
            w_all = jnp.where(own, _dot(ws[0], sca_ref[...]), 0.0).astype(BF16)
            acc = acc + _dot(w_all, vc_ref[...].reshape(PAGE_ROWS, HEAD_DIM).astype(BF16))
        carry_ref[...] = carry
        acc_ref[...] = acc

    @pl.when(p == pl.num_programs(1) - 1)
    def _():
        o_ref[...] = (acc_ref[...] * _silu(g_ref[...])).astype(o_ref.dtype)


def _attn_sample(page_table, q, k_new, v_new, gate, bias_rows, tmat, cache_k, cache_v, layer):
    gather, scatter = _page_select_matrices()
    per_b = pl.BlockSpec((None, N_HEADS_C, HEAD_DIM), lambda b, p, pt: (b, 0, 0))
    const = lambda shape: pl.BlockSpec(shape, lambda b, p, pt: (0, 0))
    assert N_PAGES % PAGES_PER_STEP == 0

    def cache(r):
        return pl.BlockSpec(
            (None, None, PAGE_SIZE, N_HEADS_C, HEAD_DIM),
            lambda b, p, pt: (layer, pt[b, N_PAGES - 1 - r - PAGES_PER_STEP * (jnp.maximum(p, 1) - 1)], 0, 0, 0))

    caches = [cache(r) for r in range(PAGES_PER_STEP)]
    grid_spec = pltpu.PrefetchScalarGridSpec(
        num_scalar_prefetch=1,
        grid=(DEC_BATCH, N_PAGES // PAGES_PER_STEP + 1),
        in_specs=[per_b, per_b, per_b, per_b, const((N_HEADS_C, LANES)), const((2 * LANES, 2 * LANES)),
                  const((PAGE_ROWS, PAGE_SIZE)), const((PAGE_SIZE, PAGE_ROWS))] + caches + caches,
        out_specs=per_b,
        scratch_shapes=[pltpu.VMEM((N_HEADS_C, HEAD_DIM), F32), pltpu.VMEM((N_HEADS_C, LANES), F32)],
    )
    return pl.pallas_call(
        _attn_sample_kernel,
        grid_spec=grid_spec,
        out_shape=jax.ShapeDtypeStruct((DEC_BATCH, N_HEADS_C, HEAD_DIM), BF16),
        compiler_params=pltpu.CompilerParams(dimension_semantics=("parallel", "arbitrary"),
                                             vmem_limit_bytes=VMEM_LIMIT),
        name="attn_sample",
    )(page_table, q, k_new, v_new, gate, bias_rows, tmat, gather, scatter, *([cache_k] * PAGES_PER_STEP),
      *([cache_v] * PAGES_PER_STEP))


TM = 1024
TN = 512
TQ = 512
RMS_ROWS = 256


def _even_layer(xp, xs, state, i, norm_g, w_in, a_ln_g, a_ln_b, a_ws, a_wb, b_dw, b_dw_bias, b_ln_g, b_ln_b, b_pw,
                w_out):
    a_wbt = a_wb[i].T
    ws0 = jnp.repeat(a_ws[i][:, 0, 0], A_GROUP_CH).reshape(1, W_A)
    wb0 = jnp.repeat(a_wb[i][:, 0], A_GROUP_CH).reshape(1, W_A)
    hp = _rmsnorm(xp, norm_g[i], RMS_ROWS)
    hs = _rmsnorm(xs, norm_g[i], DEC_BATCH)
    zp, zs = _matmul([hp], [hs], w_in, i, [0], 0, EVEN_IN, tm=TM, tn=TN, epilogue=_ep_plain, name="even_in")
    a_p, c_p, tail = _even_mid_prompt(zp, a_ln_g[i], a_ln_b[i], a_ws[i], a_wbt, b_dw[i], b_dw_bias[i], b_ln_g[i],
                                      b_ln_b[i])
    a_s, c_s, v_n, new_state = _even_mid_sample(zs, state[i], a_ln_g[i], a_ln_b[i], ws0, wb0, b_dw[i], b_dw_bias[i],
                                                b_ln_g[i], b_ln_b[i])
    b_p, b_s = _matmul([c_p], [c_s], b_pw, i, [0], 0, W_B, tm=TM, tn=TN, epilogue=_ep_gate,
                       row_extras=[(zp, zs, 3 * W_A + 2 * W_B)], out_dtype=BF16, name="even_pw")
    xp_new, xs_new = _matmul([a_p, b_p], [a_s, b_s], w_out, i, [0, 1], 0, D_MODEL, tm=TM, tn=TN, epilogue=_ep_resid,
                             row_extras=[(xp, xs, 0)], name="even_out")
    return xp_new, xs_new, v_n, tail, new_state


def _odd_layer(xp, xs, cache_k, cache_v, page_table, i, norm_g, w_in, q_norm, k_norm, c_bias, w_out, tmat):
    qg = q_norm[i].reshape(1, HEAD_DIM)
    kg = k_norm[i].reshape(1, HEAD_DIM)
    hp = _rmsnorm(xp, norm_g[i], RMS_ROWS)
    hs = _rmsnorm(xs, norm_g[i], DEC_BATCH)
    mm = functools.partial(_matmul, [hp], [hs], w_in, i, [0], tm=TM, tn=TN)
    q_p, q_s = mm(0, W_C, epilogue=functools.partial(_ep_headnorm, mult=SCALE), params=[qg], out_dtype=BF16,
                  name="odd_q")
    g_p, g_s = mm(3 * W_C, W_C, epilogue=_ep_plain, name="odd_g")
    k_p, k_s = mm(W_C, W_C, epilogue=functools.partial(_ep_headnorm, mult=1.0), params=[kg], name="odd_k")
    v_p, v_s = mm(2 * W_C, W_C, epilogue=_ep_plain, name="odd_v")
    o_p = _attn_prompt(q_p, k_p, v_p, g_p, c_bias[i].astype(F32), tmat, TQ)
    r3 = lambda a: a.reshape(DEC_BATCH, N_HEADS_C, HEAD_DIM)
    bias_rows = jnp.broadcast_to(c_bias[i].astype(F32)[:, None], (N_HEADS_C, LANES))
    o_s = _attn_sample(page_table, r3(q_s), r3(k_s), r3(v_s), r3(g_s), bias_rows, tmat, cache_k, cache_v, i)
    xp_new, xs_new = _matmul([o_p], [o_s.reshape(DEC_BATCH, W_C)], w_out, i, [0], 0, D_MODEL, tm=TM, tn=TN,
                             epilogue=_ep_resid, row_extras=[(xp, xs, 0)], name="odd_out")
    return xp_new, k_p, v_p, xs_new, k_s, v_s


def kernel(x_prompt, x_sample, state_conv, cache_k, cache_v, page_table, norm_even, w_in_even, a_ln_g, a_ln_b, a_ws,
           a_wb, b_dw, b_dw_bias, b_ln_g, b_ln_b, b_pw, w_out_even, norm_odd, w_in_odd, q_norm, k_norm, c_bias,
           w_out_odd):
    xp = x_prompt.reshape(M_PROMPT, D_MODEL)
    xs = x_sample.reshape(DEC_BATCH * DEC_SEQ, D_MODEL)
    tmat = _suffix_sum_matrix()
    a_v_s, conv_p, conv_s, k_p, v_p, k_s, v_s = [], [], [], [], [], [], []
    for layer in range(DEPTH):
        i = layer // 2
        if layer % 2 == 0:
            xp, xs, v_n, tail, new_state = _even_layer(xp, xs, state_conv, i, norm_even, w_in_even, a_ln_g, a_ln_b,
                                                       a_ws, a_wb, b_dw, b_dw_bias, b_ln_g, b_ln_b, b_pw, w_out_even)
            a_v_s.append(v_n.reshape(DEC_BATCH, DEC_SEQ, W_A))
            conv_p.append(tail)
            conv_s.append(new_state)
        else:
            xp, kp, vp, xs, kn, vn = _odd_layer(xp, xs, cache_k, cache_v, page_table, i, norm_odd, w_in_odd, q_norm, k_norm,
                                                c_bias, w_out_odd, tmat)
            k_p.append(kp.reshape(BATCH, SEQ, N_HEADS_C, HEAD_DIM))
            v_p.append(vp.reshape(BATCH, SEQ, N_HEADS_C, HEAD_DIM))
            k_s.append(kn.reshape(DEC_BATCH, DEC_SEQ, N_HEADS_C, HEAD_DIM))
            v_s.append(vn.reshape(DEC_BATCH, DEC_SEQ, N_HEADS_C, HEAD_DIM))
    return (xp.reshape(BATCH, SEQ, D_MODEL), xs.reshape(DEC_BATCH, DEC_SEQ, D_MODEL), jnp.stack(a_v_s),
            jnp.stack(conv_p), jnp.stack(conv_s), jnp.stack(k_p), jnp.stack(v_p), jnp.stack(k_s), jnp.stack(v_s))
```
